```python
import jax, jax.numpy as jnp
from jax import lax
import numpy as np

D_MODEL = 1024
BATCH = 8
SEQ = 2048
DEPTH = 4

N_BRANCH = 4
BRANCH_W = D_MODEL // 2
RW_N = 64
RW_H = BRANCH_W // RW_N
RW_DECAY_RANK = 64
RW_A_RANK = 64
RW_GN_EPS = 64e-5
RW_SHIFT_W = 3 * BRANCH_W + RW_DECAY_RANK + RW_A_RANK
GLA_H = 4
GLA_DV = BRANCH_W // GLA_H
GLA_DK = GLA_DV // 2
GLA_GATE_RANK = 16
GLA_LOGIT_NORM = 16.0
GLA_CHUNK = 32
ML_H = 4
ML_DV = BRANCH_W // ML_H
ML_DQK = ML_DV // 2
ML_CONV = 4
ML_CHUNK = 64
STAB_INIT = -1e30
HG_H = 4
HG_DK = BRANCH_W // HG_H
HG_DV = BRANCH_W // HG_H
HG_CHUNK = 32
NORM_EPS = 1e-6

COL_SIZES = (
    RW_SHIFT_W, BRANCH_W,
    GLA_H * GLA_DK, GLA_H * GLA_DK, BRANCH_W, GLA_GATE_RANK, BRANCH_W,
    2 * ML_H * ML_DQK, BRANCH_W, ML_H, ML_H, BRANCH_W,
    HG_H * HG_DK, HG_H * HG_DK, BRANCH_W, BRANCH_W,
    N_BRANCH * D_MODEL,
)
N_COLS = sum(COL_SIZES)

kernel_name = 'hybrid_rwkv7_gla_mlstm_hgrn2_gated_merge'


def _rms_norm(x, g, eps=NORM_EPS):
    xf = x.astype(jnp.float32)
    y = xf * lax.rsqrt(jnp.mean(xf * xf, axis=-1, keepdims=True) + eps)
    return (y * g.astype(jnp.float32)).astype(x.dtype)


def _heads(t, h):
    return t.reshape(t.shape[:-1] + (h, t.shape[-1] // h))


def _merge_heads(t):
    return t.reshape(t.shape[:-2] + (t.shape[-2] * t.shape[-1],))


def _head_rms(y, g, eps=NORM_EPS):
    y = y * lax.rsqrt(jnp.mean(y * y, axis=-1, keepdims=True) + eps)
    return _merge_heads(y) * g


def _head_ln(y, eps):
    mu = jnp.mean(y, axis=-1, keepdims=True)
    d = y - mu
    return _merge_heads(d * lax.rsqrt(jnp.mean(d * d, axis=-1, keepdims=True) + eps))


def _token_shift(t):
    return jnp.pad(t[:, :-1], ((0, 0), (1, 0), (0, 0)))


def _causal_conv(t, w):
    k, ch = w.shape
    return lax.conv_general_dilated(t, w[:, None, :].astype(t.dtype), window_strides=(1,),
                                    padding=[(k - 1, 0)], dimension_numbers=('NWC', 'WIO', 'NWC'),
                                    feature_group_count=ch)


def _to_chunks(t, c):
    b, s, h, x = t.shape
    return t.reshape(b, s // c, c, h, x).transpose(1, 0, 3, 2, 4)


def _from_chunks(t):
    n, b, h, c, x = t.shape
    return t.transpose(1, 0, 3, 2, 4).reshape(b, n * c, h, x)


def _chunk_gla(q, k, v, log_g, chunk):
    bsz, _, h, dk = q.shape
    dv = v.shape[-1]
    causal = jnp.tril(jnp.ones((chunk, chunk), dtype=bool))

    def step(state, blk):
        qc, kc, vc, gc = blk
        b = jnp.cumsum(gc, axis=2)
        rel = jnp.where(causal[:, :, None], b[:, :, :, None, :] - b[:, :, None, :, :], -jnp.inf)
        attn = jnp.einsum('bhik,bhjk,bhijk->bhij', qc, kc, jnp.exp(rel))
        o = attn @ vc + jnp.einsum('bhik,bhkv->bhiv', qc * jnp.exp(b), state)
        b_last = b[:, :, -1]
        state = (state * jnp.exp(b_last)[..., None]
                 + jnp.einsum('bhjk,bhjv->bhkv', kc * jnp.exp(b_last[:, :, None] - b), vc))
        return state, o

    init = jnp.zeros((bsz, h, dk, dv), q.dtype)
    _, o = lax.scan(step, init, tuple(_to_chunks(t, chunk) for t in (q, k, v, log_g)))
    return _from_chunks(o)


def _chunk_mlstm(q, k, v, i_pre, log_f, chunk):
    bsz, _, h, dk = q.shape
    dv = v.shape[-1]
    causal = jnp.tril(jnp.ones((chunk, chunk), dtype=bool))

    def step(carry, blk):
        c_mat, n_vec, m = carry
        qc, kc, vc, ic, fc = blk
        cf = jnp.cumsum(fc, axis=-1)
        log_d = jnp.where(causal, cf[..., :, None] - cf[..., None, :] + ic[..., None, :], -jnp.inf)
        log_inter = cf + m[..., None]
        m_t = jnp.maximum(log_inter, jnp.max(log_d, axis=-1))
        s = jnp.einsum('bhik,bhjk->bhij', qc, kc) * jnp.exp(log_d - m_t[..., None])
        w_inter = jnp.exp(log_inter - m_t)
        num = s @ vc + w_inter[..., None] * jnp.einsum('bhik,bhkv->bhiv', qc, c_mat)
        den = jnp.sum(s, axis=-1) + w_inter * jnp.einsum('bhik,bhk->bhi', qc, n_vec)
        o = num / jnp.maximum(jnp.abs(den), jnp.exp(-m_t))[..., None]
        m_new = m_t[..., -1]
        w_carry = jnp.exp(cf[..., -1] + m - m_new)
        kw = kc * jnp.exp(cf[..., -1:] - cf + ic - m_new[..., None])[..., None]
        c_mat = w_carry[..., None, None] * c_mat + jnp.einsum('bhjk,bhjv->bhkv', kw, vc)
        n_vec = w_carry[..., None] * n_vec + jnp.sum(kw, axis=2)
        return (c_mat, n_vec, m_new), o

    init = (jnp.zeros((bsz, h, dk, dv), q.dtype), jnp.zeros((bsz, h, dk), q.dtype),
            jnp.full((bsz, h), STAB_INIT, q.dtype))
    blks = (_to_chunks(q, chunk), _to_chunks(k, chunk), _to_chunks(v, chunk),
            _to_chunks(i_pre[..., None], chunk)[..., 0], _to_chunks(log_f[..., None], chunk)[..., 0])
    _, o = lax.scan(step, init, blks)
    return _from_chunks(o)


def _rwkv7(pre, mu, w0, w_up, a0, a_up, k_k, k_a, r_k, ln_g, ln_b):
    pre = pre + (_token_shift(pre) - pre) * mu
    r, k, v, w_code, a_code = jnp.split(
        pre, np.cumsum([BRANCH_W, BRANCH_W, BRANCH_W, RW_DECAY_RANK]), axis=-1)
    w = -jax.nn.softplus(-(w0 + jnp.tanh(w_code) @ w_up)) - 0.5
    decay = jnp.exp(-jnp.exp(w))
    a = jax.nn.sigmoid(a0 + a_code @ a_up)
    kk = _heads(k * k_k, RW_H)
    kk = kk / jnp.maximum(jnp.sqrt(jnp.sum(kk * kk, axis=-1, keepdims=True)), 1e-12)
    k = k * (1 + (a - 1) * k_a)
    r, k, v, a, decay = (_heads(t, RW_H) for t in (r, k, v, a, decay))

    def step(state, inp):
        r_t, w_t, k_t, v_t, ak_t, bk_t = inp
        sa = jnp.einsum('bhvk,bhk->bhv', state, ak_t)
        state = (state * w_t[:, :, None, :] + sa[..., None] * bk_t[:, :, None, :]
                 + v_t[..., None] * k_t[:, :, None, :])
        return state, jnp.einsum('bhvk,bhk->bhv', state, r_t)

    tm = lambda t: jnp.moveaxis(t, 1, 0)
    bsz = pre.shape[0]
    init = jnp.zeros((bsz, RW_H, RW_N, RW_N), pre.dtype)
    _, y = lax.scan(step, init, (tm(r), tm(decay), tm(k), tm(v), tm(-kk), tm(kk * a)))
    y = _head_ln(jnp.moveaxis(y, 0, 1), RW_GN_EPS) * ln_g + ln_b
    bonus = jnp.sum(r * k * _heads(r_k, RW_H), axis=-1, keepdims=True) * v
    return y + _merge_heads(bonus)


def _gla(q, k, v, g_code, gk_up, gk_b, norm_g):
    log_a = jax.nn.log_sigmoid(g_code @ gk_up + gk_b) / GLA_LOGIT_NORM
    o = _chunk_gla(_heads(q, GLA_H) * GLA_DK ** -0.5, _heads(k, GLA_H), _heads(v, GLA_H),
                   _heads(log_a, GLA_H), GLA_CHUNK)
    return _head_rms(o, norm_g)


def _mlstm(qk, v, i_raw, f_raw, conv_w, i_b, f_b, norm_g):
    qk = jax.nn.silu(_causal_conv(qk, conv_w))
    q, k = jnp.split(qk, 2, axis=-1)
    o = _chunk_mlstm(_heads(q, ML_H), _heads(k * ML_DQK ** -0.5, ML_H), _heads(v, ML_H),
                     i_raw + i_b, jax.nn.log_sigmoid(f_raw + f_b), ML_CHUNK)
    return _head_ln(o, NORM_EPS) * norm_g


def _hgrn2(q, f_raw, i, lb, norm_g):
    g = lb + (1 - lb) * jax.nn.sigmoid(f_raw)
    o = _chunk_gla(_heads(jax.nn.silu(q), HG_H), _heads(1 - g, HG_H), _heads(i, HG_H),
                   _heads(jnp.log(g), HG_H), HG_CHUNK)
    return _head_rms(o, norm_g)


def setup_inputs(seed: int = 0) -> dict:
    key = jax.random.key(seed)
    ks = iter(jax.random.split(key, 40))
    f32 = jnp.float32
    L, D, BW = DEPTH, D_MODEL, BRANCH_W
    nrm = lambda shape, s: s * jax.random.normal(next(ks), shape, f32)
    uni = lambda shape, lo, hi: jax.random.uniform(next(ks), shape, f32, lo, hi)
    return {
        'x': nrm((BATCH, SEQ, D), 1.0),
        'c': nrm((BATCH, D), 1.0),
        'norm_g': 1.0 + nrm((L, D), 0.02),
        'ada_w': nrm((L, D, 3 * D), D ** -0.5),
        'ada_b': nrm((L, 3 * D), 0.02),
        'w_in': nrm((L, D, N_COLS), D ** -0.5),
        'rw_mu': uni((L, RW_SHIFT_W), 0.0, 1.0),
        'rw_w0': uni((L, BW), -3.0, 1.0),
        'rw_w_up': nrm((L, RW_DECAY_RANK, BW), 0.1),
        'rw_a0': nrm((L, BW), 0.1),
        'rw_a_up': nrm((L, RW_A_RANK, BW), RW_A_RANK ** -0.5),
        'rw_k_k': 0.85 + nrm((L, BW), 0.02),
        'rw_k_a': 1.0 + nrm((L, BW), 0.02),
        'rw_r_k': nrm((L, BW), 0.1),
        'rw_ln_g': 1.0 + nrm((L, BW), 0.02),
        'rw_ln_b': nrm((L, BW), 0.02),
        'gla_gk_up': nrm((L, GLA_GATE_RANK, GLA_H * GLA_DK), GLA_GATE_RANK ** -0.5),
        'gla_gk_b': nrm((L, GLA_H * GLA_DK), 0.1),
        'gla_norm_g': 1.0 + nrm((L, BW), 0.02),
        'ml_conv_w': nrm((L, ML_CONV, 2 * ML_H * ML_DQK), ML_CONV ** -0.5),
        'ml_i_b': nrm((L, ML_H), 0.1),
        'ml_f_b': jnp.linspace(3.0, 6.0, ML_H, dtype=f32)[None, :] + nrm((L, ML_H), 0.1),
        'ml_norm_g': 1.0 + nrm((L, BW), 0.02),
        'hg_lb_logits': nrm((L, HG_H * HG_DK), 0.5),
        'hg_norm_g': 1.0 + nrm((L, BW), 0.02),
        'w_branch': nrm((L, N_BRANCH, BW, D), BW ** -0.5),
        'w_out': nrm((L, D, D), D ** -0.5),
        'final_g': 1.0 + nrm((D,), 0.02),
    }


def reference(x, c, norm_g, ada_w, ada_b, w_in, rw_mu, rw_w0, rw_w_up, rw_a0, rw_a_up,
              rw_k_k, rw_k_a, rw_r_k, rw_ln_g, rw_ln_b, gla_gk_up, gla_gk_b, gla_norm_g,
              ml_conv_w, ml_i_b, ml_f_b, ml_norm_g, hg_lb_logits, hg_norm_g, w_branch, w_out,
              final_g):
    dt = x.dtype
    f32 = jnp.float32
    lb_p = jax.nn.softmax(hg_lb_logits.astype(f32), axis=0)
    lower_bounds = jnp.cumsum(lb_p, axis=0) - lb_p[0]
    cond = jax.nn.silu(c)
    split_at = np.cumsum(COL_SIZES)[:-1]
    for l in range(DEPTH):
        shift, scale, gate = jnp.split(cond @ ada_w[l] + ada_b[l], 3, axis=-1)
        u = _rms_norm(x, norm_g[l]) * (1 + scale[:, None]) + shift[:, None]
        p = u @ w_in[l]
        (rw_pre, rw_z, gq, gk, gv, g_code, gz, mqk, mv, mi, mf, mz,
         hq, hf, hi, hz, merge_logits) = [t.astype(f32) for t in jnp.split(p, split_at, axis=-1)]
        y_a = _rwkv7(rw_pre, rw_mu[l], rw_w0[l], rw_w_up[l], rw_a0[l], rw_a_up[l], rw_k_k[l],
                     rw_k_a[l], rw_r_k[l], rw_ln_g[l], rw_ln_b[l])
        y_b = _gla(gq, gk, gv, g_code, gla_gk_up[l], gla_gk_b[l], gla_norm_g[l])
        y_c = _mlstm(mqk, mv, mi, mf, ml_conv_w[l], ml_i_b[l], ml_f_b[l], ml_norm_g[l])
        y_d = _hgrn2(hq, hf, hi, lower_bounds[l], hg_norm_g[l])
        ys = jnp.stack([y_a * jax.nn.silu(rw_z), y_b * jax.nn.silu(gz),
                        y_c * jax.nn.silu(mz), y_d * jax.nn.silu(hz)], axis=2).astype(dt)
        branch = jnp.einsum('bsmv,mvd->bsmd', ys, w_branch[l])
        gates = jax.nn.sigmoid(merge_logits).reshape(merge_logits.shape[:-1] + (N_BRANCH, D_MODEL))
        merged = jnp.einsum('bsmd,bsmd->bsd', gates.astype(dt), branch)
        x = x + gate[:, None] * (merged @ w_out[l])
    return _rms_norm(x, final_g)
```

```python
import functools

import jax
import jax.numpy as jnp
import numpy as np
from jax import lax
from jax.experimental import pallas as pl
from jax.experimental.pallas import tpu as pltpu

F32 = jnp.float32
BF16 = jnp.bfloat16

D_MODEL = 1024
N_BRANCH = 4
BRANCH_W = 512
RW_N = 64
RW_H = 8
RW_RANK = 64
RW_GN_EPS = 64e-5
GLA_H = 4
GLA_DK = 64
GLA_RANK = 16
GLA_LOGIT_NORM = 16.0
ML_H = 4
ML_DQK = 64
ML_CONV = 4
STAB_INIT = -1e30
HG_H = 4
NORM_EPS = 1e-6

LANES = 128
TILE_W = 512
MIX_T = 256
RW_CHUNK = 64
GLA_CHUNK = 64
HG_CHUNK = 32
SAFE_EXP = 80.0
VMEM_LIMIT = 56 * 1024 * 1024

T_RW_R, T_RW_K, T_RW_V, T_RW_Z, T_SMALL = 0, 1, 2, 3, 4
T_GLA_QK, T_GLA_V, T_GLA_Z = 5, 6, 7
T_ML_QK, T_ML_V, T_ML_Z = 8, 9, 10
T_HG_Q, T_HG_F, T_HG_I, T_HG_Z = 11, 12, 13, 14
T_MERGE = 15
N_TILES = 23
SM_GCODE = 128
SM_MI = 144
SM_MF = 148


def _cparams(sem):
    return pltpu.CompilerParams(dimension_semantics=sem, vmem_limit_bytes=VMEM_LIMIT)


def _sigmoid(x):
    return 1.0 / (1.0 + jnp.exp(-x))


def _silu(x):
    return x * _sigmoid(x)


def _softplus(x):
    return jnp.maximum(x, 0.0) + jnp.log(1.0 + jnp.exp(-jnp.abs(x)))


def _log_sigmoid(x):
    return -_softplus(-x)


def _split3(x):
    hi = x.astype(BF16)
    r1 = x - hi.astype(F32)
    mid = r1.astype(BF16)
    lo = (r1 - mid.astype(F32)).astype(BF16)
    return hi, mid, lo


def _dot(a, b):
    return jnp.dot(a, b, preferred_element_type=F32)


def _dot_nt(a, b):
    return lax.dot_general(a, b, (((1,), (1,)), ((), ())), preferred_element_type=F32)


def _mask_dot_left(m01, x):
    hi, mid, lo = _split3(x)
    return _dot(m01, hi) + _dot(m01, mid) + _dot(m01, lo)


def _mask_dot_right(x, m01):
    hi, mid, lo = _split3(x)
    return _dot(hi, m01) + _dot(mid, m01) + _dot(lo, m01)


def _iota2(shape, dim):
    return lax.broadcasted_iota(jnp.int32, shape, dim)


def _chunk_masks(t, c):
    sh = int(np.log2(c))
    ri = _iota2((t, t), 0)
    ci = _iota2((t, t), 1)
    same = (ri >> sh) == (ci >> sh)
    return ri, ci, same


def _seg_ones(width, seg):
    sh = int(np.log2(seg))
    ri = _iota2((width, width), 0)
    ci = _iota2((width, width), 1)
    return jnp.where((ri >> sh) == (ci >> sh), 1.0, 0.0).astype(BF16)


def _ada_kernel(c_ref, w_ref, b_ref, o_ref):
    cond = _silu(c_ref[...]).astype(BF16)
    o_ref[...] = _dot(cond, w_ref[...].astype(BF16)) + b_ref[...]


def _ada_call(c, ada_w, ada_b):
    depth, d, n3 = ada_w.shape
    bsz = c.shape[0]
    tn = 1024
    return pl.pallas_call(
        _ada_kernel,
        grid=(depth, n3 // tn),
        in_specs=[
            pl.BlockSpec((bsz, d), lambda l, j: (0, 0)),
            pl.BlockSpec((None, d, tn), lambda l, j: (l, 0, j)),
            pl.BlockSpec((None, 1, tn), lambda l, j: (l, 0, j)),
        ],
        out_specs=pl.BlockSpec((None, bsz, tn), lambda l, j: (l, 0, j)),
        out_shape=jax.ShapeDtypeStruct((depth, bsz, n3), F32),
        compiler_params=_cparams(("parallel", "parallel")),
        name="ada_mod",
    )(c, ada_w, ada_b.reshape(depth, 1, n3))


def _inproj_kernel(x_ref, g_ref, sc_ref, sh_ref, w_ref, o_ref, u_ref):
    @pl.when(pl.program_id(2) == 0)
    def _():
        x = x_ref[...]
        ms = jnp.mean(x * x, axis=-1, keepdims=True)
        y = x * lax.rsqrt(ms + NORM_EPS) * g_ref[...]
        u_ref[...] = (y * (1.0 + sc_ref[...]) + sh_ref[...]).astype(BF16)

    o_ref[...] = _dot(u_ref[...], w_ref[...])


def _inproj_call(x, g, scale, shift, w_tiles):
    bsz, seq, d = x.shape
    tm = min(seq, 1024)
    return pl.pallas_call(
        _inproj_kernel,
        grid=(bsz, seq // tm, N_TILES),
        in_specs=[
            pl.BlockSpec((None, tm, d), lambda b, i, j: (b, i, 0)),
            pl.BlockSpec((1, d), lambda b, i, j: (0, 0)),
            pl.BlockSpec((None, 1, d), lambda b, i, j: (b, 0, 0)),
            pl.BlockSpec((None, 1, d), lambda b, i, j: (b, 0, 0)),
            pl.BlockSpec((None, d, TILE_W), lambda b, i, j: (j, 0, 0)),
        ],
        out_specs=pl.BlockSpec((None, None, tm, TILE_W), lambda b, i, j: (j, b, i, 0)),
        out_shape=jax.ShapeDtypeStruct((N_TILES, bsz, seq, TILE_W), F32),
        scratch_shapes=[pltpu.VMEM((tm, d), BF16)],
        compiler_params=_cparams(("parallel", "parallel", "arbitrary")),
        name="in_proj",
    )(x, g.reshape(1, d), scale.reshape(bsz, 1, d), shift.reshape(bsz, 1, d), w_tiles)


def _tile_spec(tile_id, t):
    return pl.BlockSpec((None, None, t, TILE_W), lambda b, i: (tile_id, b, i, 0))


def _vec_spec(rows, width):
    return pl.BlockSpec((rows, width), lambda b, i: (0, 0))


def _block_inverse(n0, ri, ci, c):
    t = n0.shape[0]
    eye = jnp.where(ri == ci, 1.0, 0.0)
    lvl1 = (ri >> 1) == (ci >> 1)
    x = eye + jnp.where(lvl1, n0, 0.0)
    for lv in range(2, int(np.log2(c)) + 1):
        m = ((ri >> lv) == (ci >> lv)) & ((ri >> (lv - 1)) != (ci >> (lv - 1)))
        nl = jnp.where(m, n0, 0.0).astype(BF16)
        xb = x.astype(BF16)
        x = x + _dot(_dot(xb, nl).astype(BF16), xb)
    return x


def _rwkv_kernel(r_ref, k_ref, v_ref, sm_ref, z_ref, mu_ref, vec_ref, up_ref, o_ref,
                 carry_ref, st_ref, *, t_blk, chunk):
    T, C = t_blk, chunk
    n_chunk = T // C

    @pl.when(pl.program_id(1) == 0)
    def _():
        carry_ref[...] = jnp.zeros_like(carry_ref)
        st_ref[...] = jnp.zeros_like(st_ref)

    row0 = _iota2((T, 1), 0) == 0

    def shift_mix(x, slot, width, mu):
        prev = carry_ref[slot:slot + 1, 0:width]
        xp = jnp.where(row0, prev, pltpu.roll(x, 1, 0))
        carry_ref[slot:slot + 1, 0:width] = x[T - 1:T, :]
        return x + (xp - x) * mu

    r = shift_mix(r_ref[...], 0, TILE_W, mu_ref[0:1, :])
    k = shift_mix(k_ref[...], 1, TILE_W, mu_ref[1:2, :])
    v = shift_mix(v_ref[...], 2, TILE_W, mu_ref[2:3, :])
    codes = shift_mix(sm_ref[:, 0:LANES], 3, LANES, mu_ref[3:4, 0:LANES])

    w0, a0 = vec_ref[0:1, :], vec_ref[1:2, :]
    k_k, k_a, r_k = vec_ref[2:3, :], vec_ref[3:4, :], vec_ref[4:5, :]
    ln_g, ln_b = vec_ref[5:6, :], vec_ref[6:7, :]

    lane = _iota2((T, LANES), 1)
    codes_t = jnp.where(lane < RW_RANK, jnp.tanh(codes), codes).astype(BF16)
    proj = _dot(codes_t, up_ref[...])
    w = -_softplus(-(w0 + proj[:, 0:TILE_W])) - 0.5
    lw = -jnp.exp(w)
    a = _sigmoid(a0 + proj[:, TILE_W:2 * TILE_W])

    ones64 = _seg_ones(TILE_W, RW_N)
    kkr = k * k_k
    ss = _mask_dot_right(kkr * kkr, ones64)
    kk = kkr / jnp.maximum(jnp.sqrt(ss), 1e-12)
    k2 = k * (1.0 + (a - 1.0) * k_a)
    alpha = -kk
    beta = kk * a

    ri, ci, same = _chunk_masks(T, C)
    tri = jnp.where(same & (ci <= ri), 1.0, 0.0).astype(BF16)
    blk = jnp.where(same, 1.0, 0.0).astype(BF16)
    cw = _mask_dot_left(tri, lw)
    cwl = _mask_dot_left(blk, lw)
    e_in = jnp.exp(cw)
    e_out = jnp.exp(-cw)
    e_end = jnp.exp(cwl - cw)
    a_t = alpha * jnp.exp(cw - lw)
    r_t = r * e_in
    b_t = beta * e_out
    k_t = k2 * e_out
    b_h = (beta * e_end).astype(BF16)
    k_h = (k2 * e_end).astype(BF16)
    p_tot = jnp.exp(cwl)

    a_tt = a_t.T.astype(BF16)
    r_tt = r_t.T.astype(BF16)
    v_tt = v.T.astype(BF16)

    upper = ci >= ri
    strict = ci > ri
    yt_heads = []
    for h in range(RW_H):
        p0 = (h // 2) * LANES
        hmask = (_iota2((1, LANES), 1) >> 6) == (h % 2)
        lo = h * RW_N
        lhs = jnp.concatenate([jnp.where(hmask, b_t[:, p0:p0 + LANES], 0.0),
                               jnp.where(hmask, k_t[:, p0:p0 + LANES], 0.0)], axis=0).astype(BF16)
        rhs = jnp.concatenate([a_tt[p0:p0 + LANES, :], r_tt[p0:p0 + LANES, :]], axis=1)
        m = _dot(lhs, rhs)
        m_ba = jnp.where(same & strict, m[0:T, 0:T], 0.0)
        m_br = jnp.where(same & upper, m[0:T, T:2 * T], 0.0).astype(BF16)
        m_ka = jnp.where(same & strict, m[T:2 * T, 0:T], 0.0)
        m_kr = jnp.where(same & upper, m[T:2 * T, T:2 * T], 0.0)
        tt = _block_inverse(m_ba, ri, ci, C).astype(BF16)
        vth = v_tt[lo:lo + RW_N, :]
        g2 = _dot(vth, jnp.concatenate([m_ka, m_kr], axis=1).astype(BF16))
        wu = _dot(jnp.concatenate([a_tt[lo:lo + RW_N, :], g2[:, 0:T].astype(BF16)], axis=0), tt)
        w_t = wu[0:RW_N, :].astype(BF16)
        u0_t = wu[RW_N:2 * RW_N, :]
        y0_t = g2[:, T:2 * T]
        st = st_ref[lo:lo + RW_N, :]
        yts = []
        for c in range(n_chunk):
            c0 = c * C
            rhs1 = jnp.concatenate([w_t[:, c0:c0 + C], r_tt[lo:lo + RW_N, c0:c0 + C]], axis=1)
            d1 = _dot(st.astype(BF16), rhs1)
            ut = d1[:, 0:C] + u0_t[:, c0:c0 + C]
            rhs2 = jnp.concatenate([m_br[c0:c0 + C, c0:c0 + C], b_h[c0:c0 + C, lo:lo + RW_N]], axis=1)
            d2 = _dot(ut.astype(BF16), rhs2)
            yts.append(d1[:, C:2 * C] + y0_t[:, c0:c0 + C] + d2[:, 0:C])
            kv = _dot(vth[:, c0:c0 + C], k_h[c0:c0 + C, lo:lo + RW_N])
            st = st * p_tot[c0:c0 + 1, lo:lo + RW_N] + d2[:, C:C + RW_N] + kv
        st_ref[lo:lo + RW_N, :] = st
        yt = jnp.concatenate(yts, axis=1)
        mu = jnp.mean(yt, axis=0, keepdims=True)
        d = yt - mu
        yt_heads.append(d * lax.rsqrt(jnp.mean(d * d, axis=0, keepdims=True) + RW_GN_EPS))
    y = jnp.concatenate(yt_heads, axis=0).T * ln_g + ln_b
    bonus = _mask_dot_right(r * k2 * r_k, ones64) * v
    o_ref[...] = ((y + bonus) * _silu(z_ref[...])).astype(o_ref.dtype)


def _rwkv_call(p3, mu, vec, up):
    _, bsz, seq, _ = p3.shape
    t = min(seq, MIX_T)
    kern = functools.partial(_rwkv_kernel, t_blk=t, chunk=min(RW_CHUNK, t))
    return pl.pallas_call(
        kern,
        grid=(bsz, seq // t),
        in_specs=[_tile_spec(T_RW_R, t), _tile_spec(T_RW_K, t), _tile_spec(T_RW_V, t),
                  _tile_spec(T_SMALL, t), _tile_spec(T_RW_Z, t),
                  _vec_spec(8, TILE_W), _vec_spec(8, TILE_W), _vec_spec(LANES, 2 * TILE_W)],
        out_specs=pl.BlockSpec((None, t, BRANCH_W), lambda b, i: (b, i, 0)),
        out_shape=jax.ShapeDtypeStruct((bsz, seq, BRANCH_W), BF16),
        scratch_shapes=[pltpu.VMEM((8, TILE_W), F32), pltpu.VMEM((RW_H * RW_N, RW_N), F32)],
        compiler_params=_cparams(("parallel", "arbitrary")),
        name="rwkv7_mix",
    )(p3, p3, p3, p3, p3, mu, vec, up)


def _gla_recurrence(q, k, v, logg, st_ref, qs_ref, ks_ref, vs_ref, gs_ref, os_ref, *,
                    n_head, dk, t_blk, chunk):
    T, C = t_blk, chunk
    n_chunk = T // C
    dv = LANES

    def window(h):
        if dk == LANES:
            return h * LANES, None
        return (h // 2) * LANES, (_iota2((1, LANES), 1) >> 6) == (h % 2)

    ri, ci, same = _chunk_masks(T, C)
    tri = jnp.where(same & (ci <= ri), 1.0, 0.0).astype(BF16)
    blk = jnp.where(same, 1.0, 0.0).astype(BF16)
    b = _mask_dot_left(tri, logg)
    btot = _mask_dot_left(blk, logg)
    worst = jnp.max(jnp.max(-btot, axis=1, keepdims=True), axis=0, keepdims=True)
    fast = worst[0, 0] <= SAFE_EXP

    @pl.when(fast)
    def _():
        q_t = (q * jnp.exp(b)).astype(BF16)
        k_t = k * jnp.exp(-b)
        k_h = k * jnp.exp(btot - b)
        d_tot = jnp.exp(btot)
        v_tt = v.T.astype(BF16)
        vb = v.astype(BF16)
        causal = same & (ci <= ri)
        for h in range(n_head):
            w0, hm = window(h)
            qw = q_t[:, w0:w0 + LANES]
            ktw = k_t[:, w0:w0 + LANES]
            khw = k_h[:, w0:w0 + LANES]
            if hm is not None:
                ktw = jnp.where(hm, ktw, 0.0)
                khw = jnp.where(hm, khw, 0.0)
            ktw = ktw.astype(BF16)
            khw = khw.astype(BF16)
            att = jnp.where(causal, _dot_nt(qw, ktw), 0.0).astype(BF16)
            o_h = _dot(att, vb[:, h * dv:(h + 1) * dv])
            st = st_ref[h * dv:(h + 1) * dv, :]
            inter = []
            for c in range(n_chunk):
                c0 = c * C
                inter.append(_dot_nt(qw[c0:c0 + C, :], st.astype(BF16)))
                ds = _dot(v_tt[h * dv:(h + 1) * dv, c0:c0 + C], khw[c0:c0 + C, :])
                st = st * d_tot[c0:c0 + 1, w0:w0 + LANES] + ds
            st_ref[h * dv:(h + 1) * dv, :] = st
            os_ref[:, h * dv:(h + 1) * dv] = o_h + jnp.concatenate(inter, axis=0)

    @pl.when(jnp.logical_not(fast))
    def _():
        qs_ref[...] = q
        ks_ref[...] = k
        gs_ref[...] = logg
        vs_ref[...] = v
        eye = _iota2((LANES, LANES), 0) == _iota2((LANES, LANES), 1)
        for h in range(n_head):
            w0, hm = window(h)

            def body(t8, st, w0=w0, hm=hm, h=h):
                rows = pl.ds(pl.multiple_of(t8 * 8, 8), 8)
                q8 = qs_ref[rows, w0:w0 + LANES]
                k8 = ks_ref[rows, w0:w0 + LANES]
                e8 = jnp.exp(gs_ref[rows, w0:w0 + LANES])
                v8 = vs_ref[rows, h * dv:(h + 1) * dv]
                if hm is not None:
                    k8 = jnp.where(hm, k8, 0.0)
                    q8 = jnp.where(hm, q8, 0.0)
                o_rows = []
                for j in range(8):
                    v_col = jnp.sum(jnp.where(eye, v8[j:j + 1, :], 0.0), axis=1, keepdims=True)
                    st = st * e8[j:j + 1, :] + v_col * k8[j:j + 1, :]
                    o_col = jnp.sum(st * q8[j:j + 1, :], axis=1, keepdims=True)
                    o_rows.append(jnp.sum(jnp.where(eye, o_col, 0.0), axis=0, keepdims=True))
                os_ref[rows, h * dv:(h + 1) * dv] = jnp.concatenate(o_rows, axis=0)
                return st

            st_ref[h * dv:(h + 1) * dv, :] = lax.fori_loop(0, T // 8, body, st_ref[h * dv:(h + 1) * dv, :])


def _head_rms_gate(o, g, z, n_head):
    outs = []
    for h in range(n_head):
        oh = o[:, h * LANES:(h + 1) * LANES]
        outs.append(oh * lax.rsqrt(jnp.mean(oh * oh, axis=-1, keepdims=True) + NORM_EPS))
    return jnp.concatenate(outs, axis=1) * g * _silu(z)


def _gla_kernel(qk_ref, v_ref, sm_ref, z_ref, up_ref, vec_ref, o_ref,
                st_ref, qs_ref, ks_ref, vs_ref, gs_ref, os_ref, *, t_blk, chunk):
    @pl.when(pl.program_id(1) == 0)
    def _():
        st_ref[...] = jnp.zeros_like(st_ref)

    hk = GLA_H * GLA_DK
    q = qk_ref[:, 0:hk] * (GLA_DK ** -0.5)
    k = qk_ref[:, hk:2 * hk]
    code = sm_ref[:, SM_GCODE:SM_GCODE + LANES].astype(BF16)
    logits = _dot(code, up_ref[...]) + vec_ref[0:1, 0:hk]
    logg = _log_sigmoid(logits) / GLA_LOGIT_NORM
    _gla_recurrence(q, k, v_ref[...], logg, st_ref, qs_ref, ks_ref, vs_ref, gs_ref, os_ref,
                    n_head=GLA_H, dk=GLA_DK, t_blk=t_blk, chunk=chunk)
    o_ref[...] = _head_rms_gate(os_ref[...], vec_ref[1:2, :], z_ref[...], GLA_H).astype(o_ref.dtype)


def _hgrn_kernel(q_ref, f_ref, i_ref, z_ref, vec_ref, o_ref,
                 st_ref, qs_ref, ks_ref, vs_ref, gs_ref, os_ref, *, t_blk, chunk):
    @pl.when(pl.program_id(1) == 0)
    def _():
        st_ref[...] = jnp.zeros_like(st_ref)

    lb = vec_ref[0:1, :]
    g = lb + (1.0 - lb) * _sigmoid(f_ref[...])
    _gla_recurrence(_silu(q_ref[...]), 1.0 - g, i_ref[...], jnp.log(g), st_ref,
                    qs_ref, ks_ref, vs_ref, gs_ref, os_ref,
                    n_head=HG_H, dk=LANES, t_blk=t_blk, chunk=chunk)
    o_ref[...] = _head_rms_gate(os_ref[...], vec_ref[1:2, :], z_ref[...], HG_H).astype(o_ref.dtype)


def _gla_scratch(t, key_w):
    return [pltpu.VMEM((4 * LANES, LANES), F32), pltpu.VMEM((t, key_w), F32), pltpu.VMEM((t, key_w), F32),
            pltpu.VMEM((t, BRANCH_W), F32), pltpu.VMEM((t, key_w), F32), pltpu.VMEM((t, BRANCH_W), F32)]


def _gla_call(p3, up, vec):
    _, bsz, seq, _ = p3.shape
    t = min(seq, MIX_T)
    kern = functools.partial(_gla_kernel, t_blk=t, chunk=min(GLA_CHUNK, t))
    return pl.pallas_call(
        kern,
        grid=(bsz, seq // t),
        in_specs=[_tile_spec(T_GLA_QK, t), _tile_spec(T_GLA_V, t), _tile_spec(T_SMALL, t),
                  _tile_spec(T_GLA_Z, t), _vec_spec(LANES, GLA_H * GLA_DK), _vec_spec(8, TILE_W)],
        out_specs=pl.BlockSpec((None, t, BRANCH_W), lambda b, i: (b, i, 0)),
        out_shape=jax.ShapeDtypeStruct((bsz, seq, BRANCH_W), BF16),
        scratch_shapes=_gla_scratch(t, GLA_H * GLA_DK),
        compiler_params=_cparams(("parallel", "arbitrary")),
        name="gla_mix",
    )(p3, p3, p3, p3, up, vec)


def _hgrn_call(p3, vec):
    _, bsz, seq, _ = p3.shape
    t = min(seq, MIX_T)
    kern = functools.partial(_hgrn_kernel, t_blk=t, chunk=min(HG_CHUNK, t))
    return pl.pallas_call(
        kern,
        grid=(bsz, seq // t),
        in_specs=[_tile_spec(T_HG_Q, t), _tile_spec(T_HG_F, t), _tile_spec(T_HG_I, t),
                  _tile_spec(T_HG_Z, t), _vec_spec(8, TILE_W)],
        out_specs=pl.BlockSpec((None, t, BRANCH_W), lambda b, i: (b, i, 0)),
        out_shape=jax.ShapeDtypeStruct((bsz, seq, BRANCH_W), BF16),
        scratch_shapes=_gla_scratch(t, TILE_W),
        compiler_params=_cparams(("parallel", "arbitrary")),
        name="hgrn2_mix",
    )(p3, p3, p3, p3, vec)


def _mlstm_kernel(qk_ref, v_ref, sm_ref, z_ref, conv_ref, vec_ref, o_ref,
                  prev_ref, ct_ref, n_ref, m_ref, *, t_blk):
    T = t_blk

    @pl.when(pl.program_id(1) == 0)
    def _():
        prev_ref[...] = jnp.zeros_like(prev_ref)
        ct_ref[...] = jnp.zeros_like(ct_ref)
        n_ref[...] = jnp.zeros_like(n_ref)
        m_ref[...] = jnp.full_like(m_ref, STAB_INIT)

    x = qk_ref[...]
    xx = jnp.concatenate([prev_ref[...], x], axis=0)
    prev_ref[...] = x[T - 8:T, :]
    conv = x * conv_ref[ML_CONV - 1:ML_CONV, :]
    for d in range(1, ML_CONV):
        conv = conv + xx[8 - d:8 - d + T, :] * conv_ref[ML_CONV - 1 - d:ML_CONV - d, :]
    qk = _silu(conv)
    hq = ML_H * ML_DQK
    qb = qk[:, 0:hq].astype(BF16)
    kf = qk[:, hq:2 * hq] * (ML_DQK ** -0.5)
    kb = kf.astype(BF16)
    v = v_ref[...]
    vb = v.astype(BF16)
    v_tt = v.T.astype(BF16)

    gates = sm_ref[:, SM_GCODE:SM_GCODE + LANES]
    i_col = gates + vec_ref[0:1, 0:LANES]
    lf_col = _log_sigmoid(gates + vec_ref[1:2, 0:LANES])
    ri = _iota2((T, T), 0)
    ci = _iota2((T, T), 1)
    causal = ci <= ri
    tri = jnp.where(causal, 1.0, 0.0).astype(BF16)
    cf_col = _mask_dot_left(tri, lf_col)
    cf_row = cf_col.T
    i_row = i_col.T
    li, lf = SM_MI - SM_GCODE, SM_MF - SM_GCODE

    outs = []
    for h in range(ML_H):
        p0 = (h // 2) * LANES
        hm = (_iota2((1, LANES), 1) >> 6) == (h % 2)
        cf_i = cf_col[:, lf + h:lf + h + 1]
        i_i = i_col[:, li + h:li + h + 1]
        cf_j = cf_row[lf + h:lf + h + 1, :]
        i_j = i_row[li + h:li + h + 1, :]
        m_prev = m_ref[h:h + 1, 0:1]
        log_d = jnp.where(causal, cf_i - cf_j + i_j, -jnp.inf)
        log_inter = cf_i + m_prev
        m_t = jnp.maximum(log_inter, jnp.max(log_d, axis=-1, keepdims=True))
        qh = jnp.where(hm, qb[:, p0:p0 + LANES], 0.0)
        kh = kb[:, p0:p0 + LANES]
        s = _dot_nt(qh, kh) * jnp.exp(log_d - m_t)
        w_inter = jnp.exp(log_inter - m_t)
        ct = ct_ref[h * LANES:(h + 1) * LANES, :]
        n_row = n_ref[h:h + 1, :]
        num = _dot(s.astype(BF16), vb[:, h * LANES:(h + 1) * LANES]) + w_inter * _dot_nt(qh, ct.astype(BF16))
        qf = qk[:, p0:p0 + LANES]
        den = jnp.sum(s, axis=-1, keepdims=True) + w_inter * jnp.sum(
            jnp.where(hm, qf, 0.0) * n_row, axis=-1, keepdims=True)
        outs.append(num / jnp.maximum(jnp.abs(den), jnp.exp(-m_t)))
        m_new = m_t[T - 1:T, :]
        cf_last = cf_i[T - 1:T, :]
        w_carry = jnp.exp(cf_last + m_prev - m_new)
        kw = jnp.where(hm, kf[:, p0:p0 + LANES], 0.0) * jnp.exp(cf_last - cf_i + i_i - m_new)
        ct_ref[h * LANES:(h + 1) * LANES, :] = w_carry * ct + _dot(v_tt[h * LANES:(h + 1) * LANES, :],
                                                                  kw.astype(BF16))
        n_ref[h:h + 1, :] = w_carry * n_row + jnp.sum(kw, axis=0, keepdims=True)
        m_ref[h:h + 1, :] = jnp.broadcast_to(m_new, (1, LANES))

    ln = []
    for oh in outs:
        d = oh - jnp.mean(oh, axis=-1, keepdims=True)
        ln.append(d * lax.rsqrt(jnp.mean(d * d, axis=-1, keepdims=True) + NORM_EPS))
    o_ref[...] = (jnp.concatenate(ln, axis=1) * vec_ref[2:3, :] * _silu(z_ref[...])).astype(o_ref.dtype)


def _mlstm_call(p3, conv_w, vec):
    _, bsz, seq, _ = p3.shape
    t = min(seq, MIX_T)
    kern = functools.partial(_mlstm_kernel, t_blk=t)
    return pl.pallas_call(
        kern,
        grid=(bsz, seq // t),
        in_specs=[_tile_spec(T_ML_QK, t), _tile_spec(T_ML_V, t), _tile_spec(T_SMALL, t),
                  _tile_spec(T_ML_Z, t), _vec_spec(8, TILE_W), _vec_spec(8, TILE_W)],
        out_specs=pl.BlockSpec((None, t, BRANCH_W), lambda b, i: (b, i, 0)),
        out_shape=jax.ShapeDtypeStruct((bsz, seq, BRANCH_W), BF16),
        scratch_shapes=[pltpu.VMEM((8, TILE_W), F32), pltpu.VMEM((ML_H * LANES, LANES), F32),
                        pltpu.VMEM((8, LANES), F32), pltpu.VMEM((8, LANES), F32)],
        compiler_params=_cparams(("parallel", "arbitrary")),
        name="mlstm_mix",
    )(p3, p3, p3, p3, conv_w, vec)


def _merge_kernel(x_ref, ya_ref, yb_ref, yc_ref, yd_ref, *rest, final):
    lg_refs = rest[0:2 * N_BRANCH]
    wb_ref, wo_ref, gate_ref, fg_ref, o_ref = rest[2 * N_BRANCH:]
    merged = None
    for m, y_ref in enumerate((ya_ref, yb_ref, yc_ref, yd_ref)):
        br = _dot(y_ref[...], wb_ref[m])
        lg = jnp.concatenate([lg_refs[2 * m][...], lg_refs[2 * m + 1][...]], axis=1)
        term = _sigmoid(lg) * br
        merged = term if merged is None else merged + term
    out = x_ref[...] + gate_ref[...] * _dot(merged.astype(BF16), wo_ref[...])
    if final:
        ms = jnp.mean(out * out, axis=-1, keepdims=True)
        out = out * lax.rsqrt(ms + NORM_EPS) * fg_ref[...]
    o_ref[...] = out


def _merge_call(x, ys, p3, w_branch, w_out, gate, final_g, final):
    bsz, seq, d = x.shape
    tm = min(seq, 512)
    y_spec = pl.BlockSpec((None, tm, BRANCH_W), lambda b, i: (b, i, 0))
    lg_specs = [pl.BlockSpec((None, None, tm, TILE_W), functools.partial(lambda b, i, j: (j, b, i, 0), j=T_MERGE + j))
                for j in range(2 * N_BRANCH)]
    return pl.pallas_call(
        functools.partial(_merge_kernel, final=final),
        grid=(bsz, seq // tm),
        in_specs=[pl.BlockSpec((None, tm, d), lambda b, i: (b, i, 0)), y_spec, y_spec, y_spec, y_spec]
        + lg_specs
        + [pl.BlockSpec((N_BRANCH, BRANCH_W, d), lambda b, i: (0, 0, 0)),
           pl.BlockSpec((d, d), lambda b, i: (0, 0)),
           pl.BlockSpec((None, 1, d), lambda b, i: (b, 0, 0)),
           pl.BlockSpec((1, d), lambda b, i: (0, 0))],
        out_specs=pl.BlockSpec((None, tm, d), lambda b, i: (b, i, 0)),
        out_shape=jax.ShapeDtypeStruct((bsz, seq, d), F32),
        compiler_params=_cparams(("parallel", "parallel")),
        name="merge_out",
    )(x, *ys, *([p3] * (2 * N_BRANCH)), w_branch, w_out, gate.reshape(bsz, 1, d), final_g.reshape(1, d))


def _pad_cols(a, width):
    return jnp.pad(a, [(0, 0)] * (a.ndim - 1) + [(0, width - a.shape[-1])])


def _rows(rows, width, n_rows=8):
    out = [_pad_cols(r.reshape(1, -1).astype(F32), width) for r in rows]
    out.append(jnp.zeros((n_rows - len(rows), width), F32))
    return jnp.concatenate(out, axis=0)


def _relayout_w_in(w):
    o = 0

    def take(n):
        nonlocal o
        s = w[:, o:o + n]
        o += n
        return s

    rw_r, rw_k, rw_v, rw_wc, rw_ac, rw_z = take(512), take(512), take(512), take(64), take(64), take(512)
    gq, gk, gv, gcode, gz = take(256), take(256), take(512), take(16), take(512)
    mqk, mv, mi, mf, mz = take(512), take(512), take(4), take(4), take(512)
    hq, hf, hi, hz = take(512), take(512), take(512), take(512)
    merge = take(N_BRANCH * D_MODEL)
    small = _pad_cols(jnp.concatenate([rw_wc, rw_ac, gcode, mi, mf], axis=1), TILE_W)
    tiles = [rw_r, rw_k, rw_v, rw_z, small, jnp.concatenate([gq, gk], axis=1), gv, gz,
             mqk, mv, mz, hq, hf, hi, hz] + [merge[:, j * TILE_W:(j + 1) * TILE_W] for j in range(8)]
    return jnp.stack(tiles, axis=0).astype(BF16)


def kernel(x, c, norm_g, ada_w, ada_b, w_in, rw_mu, rw_w0, rw_w_up, rw_a0, rw_a_up, rw_k_k, rw_k_a,
           rw_r_k, rw_ln_g, rw_ln_b, gla_gk_up, gla_gk_b, gla_norm_g, ml_conv_w, ml_i_b, ml_f_b,
           ml_norm_g, hg_lb_logits, hg_norm_g, w_branch, w_out, final_g):
    depth = w_in.shape[0]
    d = x.shape[-1]
    lb_p = jax.nn.softmax(hg_lb_logits.astype(F32), axis=0)
    lower_bounds = jnp.cumsum(lb_p, axis=0) - lb_p[0]
    mod = _ada_call(c, ada_w, ada_b)
    for l in range(depth):
        shift, scale, gate = mod[l, :, 0:d], mod[l, :, d:2 * d], mod[l, :, 2 * d:3 * d]
        p3 = _inproj_call(x, norm_g[l], scale, shift, _relayout_w_in(w_in[l]))

        mu = rw_mu[l]
        rw_mu_rows = _rows([mu[0:512], mu[512:1024], mu[1024:1536], mu[1536:1664]], TILE_W)
        rw_vec = _rows([rw_w0[l], rw_a0[l], rw_k_k[l], rw_k_a[l], rw_r_k[l], rw_ln_g[l], rw_ln_b[l]], TILE_W)
        zeros = jnp.zeros((RW_RANK, BRANCH_W), F32)
        rw_up = jnp.concatenate([jnp.concatenate([rw_w_up[l], zeros], axis=1),
                                 jnp.concatenate([zeros, rw_a_up[l]], axis=1)], axis=0).astype(BF16)
        y_a = _rwkv_call(p3, rw_mu_rows, rw_vec, rw_up)

        gla_up = jnp.pad(gla_gk_up[l], ((0, LANES - GLA_RANK), (0, 0))).astype(BF16)
        y_b = _gla_call(p3, gla_up, _rows([gla_gk_b[l], gla_norm_g[l]], TILE_W))

        ib = jnp.pad(ml_i_b[l], (SM_MI - SM_GCODE, 0))
        fb = jnp.pad(ml_f_b[l], (SM_MF - SM_GCODE, 0))
        y_c = _mlstm_call(p3, _rows(list(ml_conv_w[l]), TILE_W), _rows([ib, fb, ml_norm_g[l]], TILE_W))

        y_d = _hgrn_call(p3, _rows([lower_bounds[l], hg_norm_g[l]], TILE_W))

        x = _merge_call(x, (y_a, y_b, y_c, y_d), p3, w_branch[l].astype(BF16), w_out[l].astype(BF16),
                        gate, final_g, final=(l == depth - 1))
    return x
```

```python
import functools

import jax
import jax.numpy as jnp
import numpy as np
from jax import lax
from jax.experimental import pallas as pl
from jax.experimental.pallas import tpu as pltpu

F32 = jnp.float32
BF16 = jnp.bfloat16

D_MODEL = 1024
N_BRANCH = 4
BRANCH_W = 512
RW_N = 64
RW_H = 8
RW_RANK = 64
RW_GN_EPS = 64e-5
GLA_H = 4
GLA_DK = 64
GLA_RANK = 16
GLA_LOGIT_NORM = 16.0
ML_H = 4
ML_DQK = 64
ML_CONV = 4
STAB_INIT = -1e30
HG_H = 4
NORM_EPS = 1e-6

LANES = 128
TILE_W = 512
MIX_T = 256
RW_CHUNK = 64
GLA_CHUNK = 64
HG_CHUNK = 16
SAFE_EXP = 80.0
VMEM_LIMIT = 56 * 1024 * 1024

T_RW_R, T_RW_K, T_RW_V, T_RW_Z, T_SMALL = 0, 1, 2, 3, 4
T_GLA_QK, T_GLA_V, T_GLA_Z = 5, 6, 7
T_ML_QK, T_ML_V, T_ML_Z = 8, 9, 10
T_HG_Q, T_HG_F, T_HG_I, T_HG_Z = 11, 12, 13, 14
T_MERGE = 15
N_TILES = 23
SM_GCODE = 128
SM_MI = 144
SM_MF = 148


def _cparams(sem):
    return pltpu.CompilerParams(dimension_semantics=sem, vmem_limit_bytes=VMEM_LIMIT)


def _sigmoid(x):
    return 1.0 / (1.0 + jnp.exp(-x))


def _silu(x):
    return x * _sigmoid(x)


def _softplus(x):
    return jnp.maximum(x, 0.0) + jnp.log(1.0 + jnp.exp(-jnp.abs(x)))


def _log_sigmoid(x):
    return -_softplus(-x)


def _split3(x):
    hi = x.astype(BF16)
    r1 = x - hi.astype(F32)
    mid = r1.astype(BF16)
    lo = (r1 - mid.astype(F32)).astype(BF16)
    return hi, mid, lo


def _dot(a, b):
    return jnp.dot(a, b, preferred_element_type=F32)


def _dot_nt(a, b):
    return lax.dot_general(a, b, (((1,), (1,)), ((), ())), preferred_element_type=F32)


def _mask_dot_left(m01, x):
    hi, mid, lo = _split3(x)
    return _dot(m01, hi) + _dot(m01, mid) + _dot(m01, lo)


def _mask_dot_right(x, m01):
    hi, mid, lo = _split3(x)
    return _dot(hi, m01) + _dot(mid, m01) + _dot(lo, m01)


def _iota2(shape, dim):
    return lax.broadcasted_iota(jnp.int32, shape, dim)


def _chunk_masks(t, c):
    sh = int(np.log2(c))
    ri = _iota2((t, t), 0)
    ci = _iota2((t, t), 1)
    same = (ri >> sh) == (ci >> sh)
    return ri, ci, same


def _seg_ones(width, seg):
    sh = int(np.log2(seg))
    ri = _iota2((width, width), 0)
    ci = _iota2((width, width), 1)
    return jnp.where((ri >> sh) == (ci >> sh), 1.0, 0.0).astype(BF16)


def _ada_kernel(c_ref, w_ref, b_ref, o_ref):
    cond = _silu(c_ref[...]).astype(BF16)
    o_ref[...] = _dot(cond, w_ref[...].astype(BF16)) + b_ref[...]


def _ada_call(c, ada_w, ada_b):
    depth, d, n3 = ada_w.shape
    bsz = c.shape[0]
    tn = 1024
    return pl.pallas_call(
        _ada_kernel,
        grid=(depth, n3 // tn),
        in_specs=[
            pl.BlockSpec((bsz, d), lambda l, j: (0, 0)),
            pl.BlockSpec((None, d, tn), lambda l, j: (l, 0, j)),
            pl.BlockSpec((None, 1, tn), lambda l, j: (l, 0, j)),
        ],
        out_specs=pl.BlockSpec((None, bsz, tn), lambda l, j: (l, 0, j)),
        out_shape=jax.ShapeDtypeStruct((depth, bsz, n3), F32),
        compiler_params=_cparams(("parallel", "parallel")),
        name="ada_mod",
    )(c, ada_w, ada_b.reshape(depth, 1, n3))


def _inproj_kernel(x_ref, g_ref, sc_ref, sh_ref, w_ref, o_ref, u_ref):
    @pl.when(pl.program_id(2) == 0)
    def _():
        x = x_ref[...]
        ms = jnp.mean(x * x, axis=-1, keepdims=True)
        y = x * lax.rsqrt(ms + NORM_EPS) * g_ref[...]
        u_ref[...] = (y * (1.0 + sc_ref[...]) + sh_ref[...]).astype(BF16)

    o_ref[...] = _dot(u_ref[...], w_ref[...])


def _inproj_call(x, g, scale, shift, w_tiles):
    bsz, seq, d = x.shape
    tm = min(seq, 2048)
    return pl.pallas_call(
        _inproj_kernel,
        grid=(bsz, seq // tm, N_TILES),
        in_specs=[
            pl.BlockSpec((None, tm, d), lambda b, i, j: (b, i, 0)),
            pl.BlockSpec((1, d), lambda b, i, j: (0, 0)),
            pl.BlockSpec((None, 1, d), lambda b, i, j: (b, 0, 0)),
            pl.BlockSpec((None, 1, d), lambda b, i, j: (b, 0, 0)),
            pl.BlockSpec((None, d, TILE_W), lambda b, i, j: (j, 0, 0)),
        ],
        out_specs=pl.BlockSpec((None, None, tm, TILE_W), lambda b, i, j: (j, b, i, 0)),
        out_shape=jax.ShapeDtypeStruct((N_TILES, bsz, seq, TILE_W), F32),
        scratch_shapes=[pltpu.VMEM((tm, d), BF16)],
        compiler_params=_cparams(("parallel", "parallel", "arbitrary")),
        name="in_proj",
    )(x, g.reshape(1, d), scale.reshape(bsz, 1, d), shift.reshape(bsz, 1, d), w_tiles)


def _tile_spec(tile_id, t):
    return pl.BlockSpec((None, None, t, TILE_W), lambda b, i: (tile_id, b, i, 0))


def _vec_spec(rows, width):
    return pl.BlockSpec((rows, width), lambda b, i: (0, 0))


def _block_inverse_all(n0s, ri, ci, c):
    eye = jnp.where(ri == ci, 1.0, 0.0)
    lvl1 = (ri >> 1) == (ci >> 1)
    xs = [eye + jnp.where(lvl1, n0, 0.0) for n0 in n0s]
    for lv in range(2, int(np.log2(c)) + 1):
        m = ((ri >> lv) == (ci >> lv)) & ((ri >> (lv - 1)) != (ci >> (lv - 1)))
        xbs = [x.astype(BF16) for x in xs]
        xns = [_dot(xb, jnp.where(m, n0, 0.0).astype(BF16)).astype(BF16) for xb, n0 in zip(xbs, n0s)]
        xs = [x + _dot(xn, xb) for x, xn, xb in zip(xs, xns, xbs)]
    return xs


def _rwkv_kernel(r_ref, k_ref, v_ref, sm_ref, z_ref, mu_ref, vec_ref, up_ref, o_ref,
                 carry_ref, st_ref, *, t_blk, chunk):
    T, C = t_blk, chunk
    n_chunk = T // C

    @pl.when(pl.program_id(1) == 0)
    def _():
        carry_ref[...] = jnp.zeros_like(carry_ref)
        st_ref[...] = jnp.zeros_like(st_ref)

    row0 = _iota2((T, 1), 0) == 0

    def shift_mix(x, slot, width, mu):
        prev = carry_ref[slot:slot + 1, 0:width]
        xp = jnp.where(row0, prev, pltpu.roll(x, 1, 0))
        carry_ref[slot:slot + 1, 0:width] = x[T - 1:T, :]
        return x + (xp - x) * mu

    r = shift_mix(r_ref[...], 0, TILE_W, mu_ref[0:1, :])
    k = shift_mix(k_ref[...], 1, TILE_W, mu_ref[1:2, :])
    v = shift_mix(v_ref[...], 2, TILE_W, mu_ref[2:3, :])
    codes = shift_mix(sm_ref[:, 0:LANES], 3, LANES, mu_ref[3:4, 0:LANES])

    w0, a0 = vec_ref[0:1, :], vec_ref[1:2, :]
    k_k, k_a, r_k = vec_ref[2:3, :], vec_ref[3:4, :], vec_ref[4:5, :]
    ln_g, ln_b = vec_ref[5:6, :], vec_ref[6:7, :]

    lane = _iota2((T, LANES), 1)
    codes_t = jnp.where(lane < RW_RANK, jnp.tanh(codes), codes).astype(BF16)
    proj = _dot(codes_t, up_ref[...])
    w = -_softplus(-(w0 + proj[:, 0:TILE_W])) - 0.5
    lw = -jnp.exp(w)
    a = _sigmoid(a0 + proj[:, TILE_W:2 * TILE_W])

    ones64 = _seg_ones(TILE_W, RW_N)
    kkr = k * k_k
    ss = _mask_dot_right(kkr * kkr, ones64)
    kk = kkr / jnp.maximum(jnp.sqrt(ss), 1e-12)
    k2 = k * (1.0 + (a - 1.0) * k_a)
    alpha = -kk
    beta = kk * a

    ri, ci, same = _chunk_masks(T, C)
    tri = jnp.where(same & (ci <= ri), 1.0, 0.0).astype(BF16)
    blk = jnp.where(same, 1.0, 0.0).astype(BF16)
    cw = _mask_dot_left(tri, lw)
    cwl = _mask_dot_left(blk, lw)
    e_in = jnp.exp(cw)
    e_out = jnp.exp(-cw)
    e_end = jnp.exp(cwl - cw)
    a_t = alpha * jnp.exp(cw - lw)
    r_t = r * e_in
    b_t = beta * e_out
    k_t = k2 * e_out
    b_h = (beta * e_end).astype(BF16)
    k_h = (k2 * e_end).astype(BF16)
    p_tot = jnp.exp(cwl)

    a_tt = a_t.T.astype(BF16)
    r_tt = r_t.T.astype(BF16)
    v_tt = v.T.astype(BF16)

    up_strict = same & (ci > ri)
    up_incl = same & (ci >= ri)
    lane_head = _iota2((1, LANES), 1) >> 6
    heads = range(RW_H)

    m_ba, m_br, m_k = [], [], []
    for h in heads:
        p0 = (h // 2) * LANES
        hmask = lane_head == (h % 2)
        lhs = jnp.concatenate([jnp.where(hmask, b_t[:, p0:p0 + LANES], 0.0),
                               jnp.where(hmask, k_t[:, p0:p0 + LANES], 0.0)], axis=0).astype(BF16)
        rhs = jnp.concatenate([a_tt[p0:p0 + LANES, :], r_tt[p0:p0 + LANES, :]], axis=1)
        m = _dot(lhs, rhs)
        m_ba.append(jnp.where(up_strict, m[0:T, 0:T], 0.0))
        m_br.append(jnp.where(up_incl, m[0:T, T:2 * T], 0.0).astype(BF16))
        m_k.append(jnp.concatenate([jnp.where(up_strict, m[T:2 * T, 0:T], 0.0),
                                    jnp.where(up_incl, m[T:2 * T, T:2 * T], 0.0)], axis=1).astype(BF16))
    tts = [x.astype(BF16) for x in _block_inverse_all(m_ba, ri, ci, C)]
    vth = [v_tt[h * RW_N:(h + 1) * RW_N, :] for h in heads]
    g2 = [_dot(vth[h], m_k[h]) for h in heads]
    wu = [_dot(jnp.concatenate([a_tt[h * RW_N:(h + 1) * RW_N, :], g2[h][:, 0:T].astype(BF16)], axis=0), tts[h])
          for h in heads]
    w_t = [wu[h][0:RW_N, :].astype(BF16) for h in heads]
    kv = [[_dot(vth[h][:, c * C:(c + 1) * C], k_h[c * C:(c + 1) * C, h * RW_N:(h + 1) * RW_N])
           for c in range(n_chunk)] for h in heads]

    st = [st_ref[h * RW_N:(h + 1) * RW_N, :] for h in heads]
    yts = [[] for _ in heads]
    for c in range(n_chunk):
        c0 = c * C
        d1 = [_dot(st[h].astype(BF16),
                   jnp.concatenate([w_t[h][:, c0:c0 + C], r_tt[h * RW_N:(h + 1) * RW_N, c0:c0 + C]], axis=1))
              for h in heads]
        ut = [(d1[h][:, 0:C] + wu[h][RW_N:2 * RW_N, c0:c0 + C]).astype(BF16) for h in heads]
        d2 = [_dot(ut[h], jnp.concatenate([m_br[h][c0:c0 + C, c0:c0 + C],
                                            b_h[c0:c0 + C, h * RW_N:(h + 1) * RW_N]], axis=1))
              for h in heads]
        for h in heads:
            yts[h].append(d1[h][:, C:2 * C] + g2[h][:, T + c0:T + c0 + C] + d2[h][:, 0:C])
            st[h] = st[h] * p_tot[c0:c0 + 1, h * RW_N:(h + 1) * RW_N] + d2[h][:, C:C + RW_N] + kv[h][c]
    yt_heads = []
    for h in heads:
        st_ref[h * RW_N:(h + 1) * RW_N, :] = st[h]
        yt = jnp.concatenate(yts[h], axis=1)
        d = yt - jnp.mean(yt, axis=0, keepdims=True)
        yt_heads.append(d * lax.rsqrt(jnp.mean(d * d, axis=0, keepdims=True) + RW_GN_EPS))
    y = jnp.concatenate(yt_heads, axis=0).T * ln_g + ln_b
    bonus = _mask_dot_right(r * k2 * r_k, ones64) * v
    o_ref[...] = ((y + bonus) * _silu(z_ref[...])).astype(o_ref.dtype)


def _rwkv_call(p3, mu, vec, up):
    _, bsz, seq, _ = p3.shape
    t = min(seq, MIX_T)
    kern = functools.partial(_rwkv_kernel, t_blk=t, chunk=min(RW_CHUNK, t))
    return pl.pallas_call(
        kern,
        grid=(bsz, seq // t),
        in_specs=[_tile_spec(T_RW_R, t), _tile_spec(T_RW_K, t), _tile_spec(T_RW_V, t),
                  _tile_spec(T_SMALL, t), _tile_spec(T_RW_Z, t),
                  _vec_spec(8, TILE_W), _vec_spec(8, TILE_W), _vec_spec(LANES, 2 * TILE_W)],
        out_specs=pl.BlockSpec((None, t, BRANCH_W), lambda b, i: (b, i, 0)),
        out_shape=jax.ShapeDtypeStruct((bsz, seq, BRANCH_W), BF16),
        scratch_shapes=[pltpu.VMEM((8, TILE_W), F32), pltpu.VMEM((RW_H * RW_N, RW_N), F32)],
        compiler_params=_cparams(("parallel", "arbitrary")),
        name="rwkv7_mix",
    )(p3, p3, p3, p3, p3, mu, vec, up)


def _gla_recurrence(q, k, v, logg, st_ref, qs_ref, ks_ref, vs_ref, gs_ref, os_ref, *,
                    n_head, dk, t_blk, chunk):
    T, C = t_blk, chunk
    n_chunk = T // C
    dv = LANES

    def window(h):
        if dk == LANES:
            return h * LANES, None
        return (h // 2) * LANES, (_iota2((1, LANES), 1) >> 6) == (h % 2)

    ri, ci, same = _chunk_masks(T, C)
    tri = jnp.where(same & (ci <= ri), 1.0, 0.0).astype(BF16)
    blk = jnp.where(same, 1.0, 0.0).astype(BF16)
    b = _mask_dot_left(tri, logg)
    btot = _mask_dot_left(blk, logg)
    worst = jnp.max(jnp.max(-btot, axis=1, keepdims=True), axis=0, keepdims=True)
    fast = worst[0, 0] <= SAFE_EXP

    @pl.when(fast)
    def _():
        q_t = (q * jnp.exp(b)).astype(BF16)
        k_t = k * jnp.exp(-b)
        k_h = k * jnp.exp(btot - b)
        d_tot = jnp.exp(btot)
        v_tt = v.T.astype(BF16)
        vb = v.astype(BF16)
        causal = same & (ci <= ri)
        heads = range(n_head)
        chunks = range(n_chunk)
        qw, ktw, khw = [], [], []
        for h in heads:
            w0, hm = window(h)
            qw.append(q_t[:, w0:w0 + LANES])
            kt_h, kh_h = k_t[:, w0:w0 + LANES], k_h[:, w0:w0 + LANES]
            if hm is not None:
                kt_h, kh_h = jnp.where(hm, kt_h, 0.0), jnp.where(hm, kh_h, 0.0)
            ktw.append(kt_h.astype(BF16))
            khw.append(kh_h.astype(BF16))
        att = [jnp.where(causal, _dot_nt(qw[h], ktw[h]), 0.0).astype(BF16) for h in heads]
        ds = [[_dot(v_tt[h * dv:(h + 1) * dv, c * C:(c + 1) * C], khw[h][c * C:(c + 1) * C, :]) for c in chunks]
              for h in heads]
        sts = []
        for h in heads:
            w0, _ = window(h)
            st = st_ref[h * dv:(h + 1) * dv, :]
            row = []
            for c in chunks:
                row.append(st.astype(BF16))
                st = st * d_tot[c * C:c * C + 1, w0:w0 + LANES] + ds[h][c]
            st_ref[h * dv:(h + 1) * dv, :] = st
            sts.append(row)
        for h in heads:
            inter = [_dot_nt(qw[h][c * C:(c + 1) * C, :], sts[h][c]) for c in chunks]
            os_ref[:, h * dv:(h + 1) * dv] = (_dot(att[h], vb[:, h * dv:(h + 1) * dv])
                                              + jnp.concatenate(inter, axis=0))

    @pl.when(jnp.logical_not(fast))
    def _():
        qs_ref[...] = q
        ks_ref[...] = k
        gs_ref[...] = logg
        vs_ref[...] = v
        eye = _iota2((LANES, LANES), 0) == _iota2((LANES, LANES), 1)
        for h in range(n_head):
            w0, hm = window(h)

            def body(t8, st, w0=w0, hm=hm, h=h):
                rows = pl.ds(pl.multiple_of(t8 * 8, 8), 8)
                q8 = qs_ref[rows, w0:w0 + LANES]
                k8 = ks_ref[rows, w0:w0 + LANES]
                e8 = jnp.exp(gs_ref[rows, w0:w0 + LANES])
                v8 = vs_ref[rows, h * dv:(h + 1) * dv]
                if hm is not None:
                    k8 = jnp.where(hm, k8, 0.0)
                    q8 = jnp.where(hm, q8, 0.0)
                o_rows = []
                for j in range(8):
                    v_col = jnp.sum(jnp.where(eye, v8[j:j + 1, :], 0.0), axis=1, keepdims=True)
                    st = st * e8[j:j + 1, :] + v_col * k8[j:j + 1, :]
                    o_col = jnp.sum(st * q8[j:j + 1, :], axis=1, keepdims=True)
                    o_rows.append(jnp.sum(jnp.where(eye, o_col, 0.0), axis=0, keepdims=True))
                os_ref[rows, h * dv:(h + 1) * dv] = jnp.concatenate(o_rows, axis=0)
                return st

            st_ref[h * dv:(h + 1) * dv, :] = lax.fori_loop(0, T // 8, body, st_ref[h * dv:(h + 1) * dv, :])


def _head_rms_gate(o, g, z, n_head):
    outs = []
    for h in range(n_head):
        oh = o[:, h * LANES:(h + 1) * LANES]
        outs.append(oh * lax.rsqrt(jnp.mean(oh * oh, axis=-1, keepdims=True) + NORM_EPS))
    return jnp.concatenate(outs, axis=1) * g * _silu(z)


def _gla_kernel(qk_ref, v_ref, sm_ref, z_ref, up_ref, vec_ref, o_ref,
                st_ref, qs_ref, ks_ref, vs_ref, gs_ref, os_ref, *, t_blk, chunk):
    @pl.when(pl.program_id(1) == 0)
    def _():
        st_ref[...] = jnp.zeros_like(st_ref)

    hk = GLA_H * GLA_DK
    q = qk_ref[:, 0:hk] * (GLA_DK ** -0.5)
    k = qk_ref[:, hk:2 * hk]
    code = sm_ref[:, SM_GCODE:SM_GCODE + LANES].astype(BF16)
    logits = _dot(code, up_ref[...]) + vec_ref[0:1, 0:hk]
    logg = _log_sigmoid(logits) / GLA_LOGIT_NORM
    _gla_recurrence(q, k, v_ref[...], logg, st_ref, qs_ref, ks_ref, vs_ref, gs_ref, os_ref,
                    n_head=GLA_H, dk=GLA_DK, t_blk=t_blk, chunk=chunk)
    o_ref[...] = _head_rms_gate(os_ref[...], vec_ref[1:2, :], z_ref[...], GLA_H).astype(o_ref.dtype)


def _hgrn_kernel(q_ref, f_ref, i_ref, z_ref, vec_ref, o_ref,
                 st_ref, qs_ref, ks_ref, vs_ref, gs_ref, os_ref, *, t_blk, chunk):
    @pl.when(pl.program_id(1) == 0)
    def _():
        st_ref[...] = jnp.zeros_like(st_ref)

    lb = vec_ref[0:1, :]
    g = lb + (1.0 - lb) * _sigmoid(f_ref[...])
    _gla_recurrence(_silu(q_ref[...]), 1.0 - g, i_ref[...], jnp.log(g), st_ref,
                    qs_ref, ks_ref, vs_ref, gs_ref, os_ref,
                    n_head=HG_H, dk=LANES, t_blk=t_blk, chunk=chunk)
    o_ref[...] = _head_rms_gate(os_ref[...], vec_ref[1:2, :], z_ref[...], HG_H).astype(o_ref.dtype)


def _gla_scratch(t, key_w):
    return [pltpu.VMEM((4 * LANES, LANES), F32), pltpu.VMEM((t, key_w), F32), pltpu.VMEM((t, key_w), F32),
            pltpu.VMEM((t, BRANCH_W), F32), pltpu.VMEM((t, key_w), F32), pltpu.VMEM((t, BRANCH_W), F32)]


def _gla_call(p3, up, vec):
    _, bsz, seq, _ = p3.shape
    t = min(seq, MIX_T)
    kern = functools.partial(_gla_kernel, t_blk=t, chunk=min(GLA_CHUNK, t))
    return pl.pallas_call(
        kern,
        grid=(bsz, seq // t),
        in_specs=[_tile_spec(T_GLA_QK, t), _tile_spec(T_GLA_V, t), _tile_spec(T_SMALL, t),
                  _tile_spec(T_GLA_Z, t), _vec_spec(LANES, GLA_H * GLA_DK), _vec_spec(8, TILE_W)],
        out_specs=pl.BlockSpec((None, t, BRANCH_W), lambda b, i: (b, i, 0)),
        out_shape=jax.ShapeDtypeStruct((bsz, seq, BRANCH_W), BF16),
        scratch_shapes=_gla_scratch(t, GLA_H * GLA_DK),
        compiler_params=_cparams(("parallel", "arbitrary")),
        name="gla_mix",
    )(p3, p3, p3, p3, up, vec)


def _hgrn_call(p3, vec):
    _, bsz, seq, _ = p3.shape
    t = min(seq, MIX_T)
    kern = functools.partial(_hgrn_kernel, t_blk=t, chunk=min(HG_CHUNK, t))
    return pl.pallas_call(
        kern,
        grid=(bsz, seq // t),
        in_specs=[_tile_spec(T_HG_Q, t), _tile_spec(T_HG_F, t), _tile_spec(T_HG_I, t),
                  _tile_spec(T_HG_Z, t), _vec_spec(8, TILE_W)],
        out_specs=pl.BlockSpec((None, t, BRANCH_W), lambda b, i: (b, i, 0)),
        out_shape=jax.ShapeDtypeStruct((bsz, seq, BRANCH_W), BF16),
        scratch_shapes=_gla_scratch(t, TILE_W),
        compiler_params=_cparams(("parallel", "arbitrary")),
        name="hgrn2_mix",
    )(p3, p3, p3, p3, vec)


def _mlstm_kernel(qk_ref, v_ref, sm_ref, z_ref, conv_ref, vec_ref, o_ref,
                  prev_ref, ct_ref, n_ref, m_ref, *, t_blk):
    T = t_blk

    @pl.when(pl.program_id(1) == 0)
    def _():
        prev_ref[...] = jnp.zeros_like(prev_ref)
        ct_ref[...] = jnp.zeros_like(ct_ref)
        n_ref[...] = jnp.zeros_like(n_ref)
        m_ref[...] = jnp.full_like(m_ref, STAB_INIT)

    x = qk_ref[...]
    xx = jnp.concatenate([prev_ref[...], x], axis=0)
    prev_ref[...] = x[T - 8:T, :]
    conv = x * conv_ref[ML_CONV - 1:ML_CONV, :]
    for d in range(1, ML_CONV):
        conv = conv + xx[8 - d:8 - d + T, :] * conv_ref[ML_CONV - 1 - d:ML_CONV - d, :]
    qk = _silu(conv)
    hq = ML_H * ML_DQK
    qb = qk[:, 0:hq].astype(BF16)
    kf = qk[:, hq:2 * hq] * (ML_DQK ** -0.5)
    kb = kf.astype(BF16)
    v = v_ref[...]
    vb = v.astype(BF16)
    v_tt = v.T.astype(BF16)

    gates = sm_ref[:, SM_GCODE:SM_GCODE + LANES]
    i_col = gates + vec_ref[0:1, 0:LANES]
    lf_col = _log_sigmoid(gates + vec_ref[1:2, 0:LANES])
    ri = _iota2((T, T), 0)
    ci = _iota2((T, T), 1)
    causal = ci <= ri
    tri = jnp.where(causal, 1.0, 0.0).astype(BF16)
    cf_col = _mask_dot_left(tri, lf_col)
    cf_row = cf_col.T
    i_row = i_col.T
    li, lf = SM_MI - SM_GCODE, SM_MF - SM_GCODE

    outs = []
    for h in range(ML_H):
        p0 = (h // 2) * LANES
        hm = (_iota2((1, LANES), 1) >> 6) == (h % 2)
        cf_i = cf_col[:, lf + h:lf + h + 1]
        i_i = i_col[:, li + h:li + h + 1]
        cf_j = cf_row[lf + h:lf + h + 1, :]
        i_j = i_row[li + h:li + h + 1, :]
        m_prev = m_ref[h:h + 1, 0:1]
        log_d = jnp.where(causal, cf_i - cf_j + i_j, -jnp.inf)
        log_inter = cf_i + m_prev
        m_t = jnp.maximum(log_inter, jnp.max(log_d, axis=-1, keepdims=True))
        qh = jnp.where(hm, qb[:, p0:p0 + LANES], 0.0)
        kh = kb[:, p0:p0 + LANES]
        s = _dot_nt(qh, kh) * jnp.exp(log_d - m_t)
        w_inter = jnp.exp(log_inter - m_t)
        ct = ct_ref[h * LANES:(h + 1) * LANES, :]
        n_row = n_ref[h:h + 1, :]
        num = _dot(s.astype(BF16), vb[:, h * LANES:(h + 1) * LANES]) + w_inter * _dot_nt(qh, ct.astype(BF16))
        qf = qk[:, p0:p0 + LANES]
        den = jnp.sum(s, axis=-1, keepdims=True) + w_inter * jnp.sum(
            jnp.where(hm, qf, 0.0) * n_row, axis=-1, keepdims=True)
        outs.append(num / jnp.maximum(jnp.abs(den), jnp.exp(-m_t)))
        m_new = m_t[T - 1:T, :]
        cf_last = cf_i[T - 1:T, :]
        w_carry = jnp.exp(cf_last + m_prev - m_new)
        kw = jnp.where(hm, kf[:, p0:p0 + LANES], 0.0) * jnp.exp(cf_last - cf_i + i_i - m_new)
        ct_ref[h * LANES:(h + 1) * LANES, :] = w_carry * ct + _dot(v_tt[h * LANES:(h + 1) * LANES, :],
                                                                  kw.astype(BF16))
        n_ref[h:h + 1, :] = w_carry * n_row + jnp.sum(kw, axis=0, keepdims=True)
        m_ref[h:h + 1, :] = jnp.broadcast_to(m_new, (1, LANES))

    ln = []
    for oh in outs:
        d = oh - jnp.mean(oh, axis=-1, keepdims=True)
        ln.append(d * lax.rsqrt(jnp.mean(d * d, axis=-1, keepdims=True) + NORM_EPS))
    o_ref[...] = (jnp.concatenate(ln, axis=1) * vec_ref[2:3, :] * _silu(z_ref[...])).astype(o_ref.dtype)


def _mlstm_call(p3, conv_w, vec):
    _, bsz, seq, _ = p3.shape
    t = min(seq, MIX_T)
    kern = functools.partial(_mlstm_kernel, t_blk=t)
    return pl.pallas_call(
        kern,
        grid=(bsz, seq // t),
        in_specs=[_tile_spec(T_ML_QK, t), _tile_spec(T_ML_V, t), _tile_spec(T_SMALL, t),
                  _tile_spec(T_ML_Z, t), _vec_spec(8, TILE_W), _vec_spec(8, TILE_W)],
        out_specs=pl.BlockSpec((None, t, BRANCH_W), lambda b, i: (b, i, 0)),
        out_shape=jax.ShapeDtypeStruct((bsz, seq, BRANCH_W), BF16),
        scratch_shapes=[pltpu.VMEM((8, TILE_W), F32), pltpu.VMEM((ML_H * LANES, LANES), F32),
                        pltpu.VMEM((8, LANES), F32), pltpu.VMEM((8, LANES), F32)],
        compiler_params=_cparams(("parallel", "arbitrary")),
        name="mlstm_mix",
    )(p3, p3, p3, p3, conv_w, vec)


def _merge_kernel(x_ref, ya_ref, yb_ref, yc_ref, yd_ref, *rest, final):
    lg_refs = rest[0:2 * N_BRANCH]
    wb_ref, wo_ref, gate_ref, fg_ref, o_ref = rest[2 * N_BRANCH:]
    merged = None
    for m, y_ref in enumerate((ya_ref, yb_ref, yc_ref, yd_ref)):
        br = _dot(y_ref[...], wb_ref[m])
        lg = jnp.concatenate([lg_refs[2 * m][...], lg_refs[2 * m + 1][...]], axis=1)
        term = _sigmoid(lg) * br
        merged = term if merged is None else merged + term
    out = x_ref[...] + gate_ref[...] * _dot(merged.astype(BF16), wo_ref[...])
    if final:
        ms = jnp.mean(out * out, axis=-1, keepdims=True)
        out = out * lax.rsqrt(ms + NORM_EPS) * fg_ref[...]
    o_ref[...] = out


def _merge_call(x, ys, p3, w_branch, w_out, gate, final_g, final):
    bsz, seq, d = x.shape
    tm = min(seq, 512)
    y_spec = pl.BlockSpec((None, tm, BRANCH_W), lambda b, i: (b, i, 0))
    lg_specs = [pl.BlockSpec((None, None, tm, TILE_W), functools.partial(lambda b, i, j: (j, b, i, 0), j=T_MERGE + j))
                for j in range(2 * N_BRANCH)]
    return pl.pallas_call(
        functools.partial(_merge_kernel, final=final),
        grid=(bsz, seq // tm),
        in_specs=[pl.BlockSpec((None, tm, d), lambda b, i: (b, i, 0)), y_spec, y_spec, y_spec, y_spec]
        + lg_specs
        + [pl.BlockSpec((N_BRANCH, BRANCH_W, d), lambda b, i: (0, 0, 0)),
           pl.BlockSpec((d, d), lambda b, i: (0, 0)),
           pl.BlockSpec((None, 1, d), lambda b, i: (b, 0, 0)),
           pl.BlockSpec((1, d), lambda b, i: (0, 0))],
        out_specs=pl.BlockSpec((None, tm, d), lambda b, i: (b, i, 0)),
        out_shape=jax.ShapeDtypeStruct((bsz, seq, d), F32),
        compiler_params=_cparams(("parallel", "parallel")),
        name="merge_out",
    )(x, *ys, *([p3] * (2 * N_BRANCH)), w_branch, w_out, gate.reshape(bsz, 1, d), final_g.reshape(1, d))


def _pad_cols(a, width):
    return jnp.pad(a, [(0, 0)] * (a.ndim - 1) + [(0, width - a.shape[-1])])


def _rows(rows, width, n_rows=8):
    out = [_pad_cols(r.reshape(1, -1).astype(F32), width) for r in rows]
    out.append(jnp.zeros((n_rows - len(rows), width), F32))
    return jnp.concatenate(out, axis=0)


def _relayout_w_in(w):
    o = 0

    def take(n):
        nonlocal o
        s = w[:, o:o + n]
        o += n
        return s

    rw_r, rw_k, rw_v, rw_wc, rw_ac, rw_z = take(512), take(512), take(512), take(64), take(64), take(512)
    gq, gk, gv, gcode, gz = take(256), take(256), take(512), take(16), take(512)
    mqk, mv, mi, mf, mz = take(512), take(512), take(4), take(4), take(512)
    hq, hf, hi, hz = take(512), take(512), take(512), take(512)
    merge = take(N_BRANCH * D_MODEL)
    small = _pad_cols(jnp.concatenate([rw_wc, rw_ac, gcode, mi, mf], axis=1), TILE_W)
    tiles = [rw_r, rw_k, rw_v, rw_z, small, jnp.concatenate([gq, gk], axis=1), gv, gz,
             mqk, mv, mz, hq, hf, hi, hz] + [merge[:, j * TILE_W:(j + 1) * TILE_W] for j in range(8)]
    return jnp.stack(tiles, axis=0).astype(BF16)


def kernel(x, c, norm_g, ada_w, ada_b, w_in, rw_mu, rw_w0, rw_w_up, rw_a0, rw_a_up, rw_k_k, rw_k_a,
           rw_r_k, rw_ln_g, rw_ln_b, gla_gk_up, gla_gk_b, gla_norm_g, ml_conv_w, ml_i_b, ml_f_b,
           ml_norm_g, hg_lb_logits, hg_norm_g, w_branch, w_out, final_g):
    depth = w_in.shape[0]
    d = x.shape[-1]
    lb_p = jax.nn.softmax(hg_lb_logits.astype(F32), axis=0)
    lower_bounds = jnp.cumsum(lb_p, axis=0) - lb_p[0]
    mod = _ada_call(c, ada_w, ada_b)
    for l in range(depth):
        shift, scale, gate = mod[l, :, 0:d], mod[l, :, d:2 * d], mod[l, :, 2 * d:3 * d]
        p3 = _inproj_call(x, norm_g[l], scale, shift, _relayout_w_in(w_in[l]))

        mu = rw_mu[l]
        rw_mu_rows = _rows([mu[0:512], mu[512:1024], mu[1024:1536], mu[1536:1664]], TILE_W)
        rw_vec = _rows([rw_w0[l], rw_a0[l], rw_k_k[l], rw_k_a[l], rw_r_k[l], rw_ln_g[l], rw_ln_b[l]], TILE_W)
        zeros = jnp.zeros((RW_RANK, BRANCH_W), F32)
        rw_up = jnp.concatenate([jnp.concatenate([rw_w_up[l], zeros], axis=1),
                                 jnp.concatenate([zeros, rw_a_up[l]], axis=1)], axis=0).astype(BF16)
        y_a = _rwkv_call(p3, rw_mu_rows, rw_vec, rw_up)

        gla_up = jnp.pad(gla_gk_up[l], ((0, LANES - GLA_RANK), (0, 0))).astype(BF16)
        y_b = _gla_call(p3, gla_up, _rows([gla_gk_b[l], gla_norm_g[l]], TILE_W))

        ib = jnp.pad(ml_i_b[l], (SM_MI - SM_GCODE, 0))
        fb = jnp.pad(ml_f_b[l], (SM_MF - SM_GCODE, 0))
        y_c = _mlstm_call(p3, _rows(list(ml_conv_w[l]), TILE_W), _rows([ib, fb, ml_norm_g[l]], TILE_W))

        y_d = _hgrn_call(p3, _rows([lower_bounds[l], hg_norm_g[l]], TILE_W))

        x = _merge_call(x, (y_a, y_b, y_c, y_d), p3, w_branch[l].astype(BF16), w_out[l].astype(BF16),
                        gate, final_g, final=(l == depth - 1))
    return x
```

```python
import functools

import jax
import jax.numpy as jnp
import numpy as np
from jax import lax
from jax.experimental import pallas as pl
from jax.experimental.pallas import tpu as pltpu

F32 = jnp.float32
BF16 = jnp.bfloat16

D_MODEL = 1024
N_BRANCH = 4
BRANCH_W = 512
RW_N = 64
RW_H = 8
RW_RANK = 64
RW_GN_EPS = 64e-5
GLA_H = 4
GLA_DK = 64
GLA_RANK = 16
GLA_LOGIT_NORM = 16.0
ML_H = 4
ML_DQK = 64
ML_CONV = 4
STAB_INIT = -1e30
HG_H = 4
NORM_EPS = 1e-6

LANES = 128
TILE_W = 512
MIX_T = 256
RW_CHUNK = 64
RW_PAIR_T = 128
GLA_CHUNK = 64
HG_CHUNK_FIRST = 16
HG_CHUNK = 32
ML_CHUNK = 256
SAFE_EXP = 80.0
VMEM_LIMIT = 56 * 1024 * 1024

T_RW_R, T_RW_K, T_RW_V, T_RW_Z, T_SMALL = 0, 1, 2, 3, 4
T_GLA_QK, T_GLA_V, T_GLA_Z = 5, 6, 7
T_ML_QK, T_ML_V, T_ML_Z = 8, 9, 10
T_HG_Q, T_HG_F, T_HG_I, T_HG_Z = 11, 12, 13, 14
N_MIX_TILES = 15
N_GATE_TILES = N_BRANCH * D_MODEL // TILE_W
SM_GCODE = 128
SM_MI = 144
SM_MF = 148


def _cparams(sem):
    return pltpu.CompilerParams(dimension_semantics=sem, vmem_limit_bytes=VMEM_LIMIT)


def _sigmoid(x):
    return 1.0 / (1.0 + jnp.exp(-x))


def _silu(x):
    return x * _sigmoid(x)


def _softplus(x):
    return jnp.maximum(x, 0.0) + jnp.log(1.0 + jnp.exp(-jnp.abs(x)))


def _log_sigmoid(x):
    return -_softplus(-x)


def _split3(x):
    hi = x.astype(BF16)
    r1 = x - hi.astype(F32)
    mid = r1.astype(BF16)
    lo = (r1 - mid.astype(F32)).astype(BF16)
    return hi, mid, lo


def _dot(a, b):
    return jnp.dot(a, b, preferred_element_type=F32)


def _dot_nt(a, b):
    return lax.dot_general(a, b, (((1,), (1,)), ((), ())), preferred_element_type=F32)


def _mask_dot_left(m01, x):
    hi, mid, lo = _split3(x)
    return _dot(m01, hi) + _dot(m01, mid) + _dot(m01, lo)


def _mask_dot_right(x, m01):
    hi, mid, lo = _split3(x)
    return _dot(hi, m01) + _dot(mid, m01) + _dot(lo, m01)


def _iota2(shape, dim):
    return lax.broadcasted_iota(jnp.int32, shape, dim)


def _chunk_masks(t, c):
    sh = int(np.log2(c))
    ri = _iota2((t, t), 0)
    ci = _iota2((t, t), 1)
    same = (ri >> sh) == (ci >> sh)
    return ri, ci, same


def _seg_ones(width, seg):
    sh = int(np.log2(seg))
    ri = _iota2((width, width), 0)
    ci = _iota2((width, width), 1)
    return jnp.where((ri >> sh) == (ci >> sh), 1.0, 0.0).astype(BF16)


def _rms_mod(x, g, scale, shift):
    ms = jnp.mean(x * x, axis=-1, keepdims=True)
    return x * lax.rsqrt(ms + NORM_EPS) * g * (1.0 + scale) + shift


def _ada_kernel(c_ref, w_ref, b_ref, o_ref):
    cond = _silu(c_ref[...]).astype(BF16)
    o_ref[...] = _dot(cond, w_ref[...].astype(BF16)) + b_ref[...]


def _ada_call(c, ada_w, ada_b):
    depth, d, n3 = ada_w.shape
    bsz = c.shape[0]
    tn = 1024
    return pl.pallas_call(
        _ada_kernel,
        grid=(depth, n3 // tn),
        in_specs=[
            pl.BlockSpec((bsz, d), lambda l, j: (0, 0)),
            pl.BlockSpec((None, d, tn), lambda l, j: (l, 0, j)),
            pl.BlockSpec((None, 1, tn), lambda l, j: (l, 0, j)),
        ],
        out_specs=pl.BlockSpec((None, bsz, tn), lambda l, j: (l, 0, j)),
        out_shape=jax.ShapeDtypeStruct((depth, bsz, n3), F32),
        compiler_params=_cparams(("parallel", "parallel")),
        name="ada_mod",
    )(c, ada_w, ada_b.reshape(depth, 1, n3))


def _norm_kernel(x_ref, g_ref, sc_ref, sh_ref, u_ref):
    u_ref[...] = _rms_mod(x_ref[...], g_ref[...], sc_ref[...], sh_ref[...]).astype(u_ref.dtype)


def _norm_call(x, g, scale, shift):
    bsz, seq, d = x.shape
    tm = min(seq, 1024)
    return pl.pallas_call(
        _norm_kernel,
        grid=(bsz, seq // tm),
        in_specs=[pl.BlockSpec((None, tm, d), lambda b, i: (b, i, 0)),
                  pl.BlockSpec((1, d), lambda b, i: (0, 0)),
                  pl.BlockSpec((None, 1, d), lambda b, i: (b, 0, 0)),
                  pl.BlockSpec((None, 1, d), lambda b, i: (b, 0, 0))],
        out_specs=pl.BlockSpec((None, tm, d), lambda b, i: (b, i, 0)),
        out_shape=jax.ShapeDtypeStruct((bsz, seq, d), BF16),
        compiler_params=_cparams(("parallel", "parallel")),
        name="norm_mod",
    )(x, g.reshape(1, d), scale.reshape(bsz, 1, d), shift.reshape(bsz, 1, d))


def _proj_kernel(u_ref, w_ref, o_ref, *, group, gate):
    u = u_ref[...]
    for t in range(group):
        r = _dot(u, w_ref[t])
        if gate:
            r = _sigmoid(r)
        o_ref[t] = r.astype(o_ref.dtype)


def _proj_call(u, w_tiles, layer, *, group, gate, name):
    bsz, seq, d = u.shape
    n_tiles = w_tiles.shape[1]
    tm = min(seq, 2048)
    return pl.pallas_call(
        functools.partial(_proj_kernel, group=group, gate=gate),
        grid=(bsz, seq // tm, n_tiles // group),
        in_specs=[pl.BlockSpec((None, tm, d), lambda b, i, j: (b, i, 0)),
                  pl.BlockSpec((None, group, d, TILE_W), lambda b, i, j: (layer, j, 0, 0))],
        out_specs=pl.BlockSpec((group, None, tm, TILE_W), lambda b, i, j: (j, b, i, 0)),
        out_shape=jax.ShapeDtypeStruct((n_tiles, bsz, seq, TILE_W), BF16 if gate else F32),
        compiler_params=_cparams(("parallel", "parallel", "arbitrary")),
        name=name,
    )(u, w_tiles)


def _tile_spec(tile_id, t):
    return pl.BlockSpec((None, None, t, TILE_W), lambda b, i: (tile_id, b, i, 0))


def _layer_spec(layer, rows, width):
    return pl.BlockSpec((None, rows, width), lambda b, i: (layer, 0, 0))


def _block_inverse_all(n0s, ri, ci, c):
    eye = jnp.where(ri == ci, 1.0, 0.0)
    lvl1 = (ri >> 1) == (ci >> 1)
    xs = [eye + jnp.where(lvl1, n0, 0.0) for n0 in n0s]
    for lv in range(2, int(np.log2(c)) + 1):
        m = ((ri >> lv) == (ci >> lv)) & ((ri >> (lv - 1)) != (ci >> (lv - 1)))
        xbs = [x.astype(BF16) for x in xs]
        xns = [_dot(xb, jnp.where(m, n0, 0.0).astype(BF16)).astype(BF16) for xb, n0 in zip(xbs, n0s)]
        xs = [x + _dot(xn, xb) for x, xn, xb in zip(xs, xns, xbs)]
    return xs


def _rwkv_kernel(r_ref, k_ref, v_ref, sm_ref, z_ref, mu_ref, vec_ref, up_ref, o_ref,
                 carry_ref, st_ref, *, t_blk, chunk, pair_t):
    T, C, TP = t_blk, chunk, pair_t
    n_chunk = T // C
    n_part = T // TP
    c_per_part = TP // C

    @pl.when(pl.program_id(1) == 0)
    def _():
        carry_ref[...] = jnp.zeros_like(carry_ref)
        st_ref[...] = jnp.zeros_like(st_ref)

    row0 = _iota2((T, 1), 0) == 0

    def shift_mix(x, slot, width, mu):
        prev = carry_ref[slot:slot + 1, 0:width]
        xp = jnp.where(row0, prev, pltpu.roll(x, 1, 0))
        carry_ref[slot:slot + 1, 0:width] = x[T - 1:T, :]
        return x + (xp - x) * mu

    r = shift_mix(r_ref[...], 0, TILE_W, mu_ref[0:1, :])
    k = shift_mix(k_ref[...], 1, TILE_W, mu_ref[1:2, :])
    v = shift_mix(v_ref[...], 2, TILE_W, mu_ref[2:3, :])
    codes = shift_mix(sm_ref[:, 0:LANES], 3, LANES, mu_ref[3:4, 0:LANES])

    w0, a0 = vec_ref[0:1, :], vec_ref[1:2, :]
    k_k, k_a, r_k = vec_ref[2:3, :], vec_ref[3:4, :], vec_ref[4:5, :]
    ln_g, ln_b = vec_ref[5:6, :], vec_ref[6:7, :]

    lane = _iota2((T, LANES), 1)
    codes_t = jnp.where(lane < RW_RANK, jnp.tanh(codes), codes).astype(BF16)
    proj = _dot(codes_t, up_ref[...])
    w = -_softplus(-(w0 + proj[:, 0:TILE_W])) - 0.5
    lw = -jnp.exp(w)
    a = _sigmoid(a0 + proj[:, TILE_W:2 * TILE_W])

    ones64 = _seg_ones(TILE_W, RW_N)
    kkr = k * k_k
    ss = _mask_dot_right(kkr * kkr, ones64)
    kk = kkr / jnp.maximum(jnp.sqrt(ss), 1e-12)
    k2 = k * (1.0 + (a - 1.0) * k_a)
    alpha = -kk
    beta = kk * a

    ri_t, ci_t, same_t = _chunk_masks(T, C)
    tri = jnp.where(same_t & (ci_t <= ri_t), 1.0, 0.0).astype(BF16)
    blk = jnp.where(same_t, 1.0, 0.0).astype(BF16)
    cw = _mask_dot_left(tri, lw)
    cwl = _mask_dot_left(blk, lw)
    e_in = jnp.exp(cw)
    e_out = jnp.exp(-cw)
    e_end = jnp.exp(cwl - cw)
    a_t = alpha * jnp.exp(cw - lw)
    r_t = r * e_in
    b_t = beta * e_out
    k_t = k2 * e_out
    b_h = (beta * e_end).astype(BF16)
    k_h = (k2 * e_end).astype(BF16)
    p_tot = jnp.exp(cwl)

    a_tt = a_t.T.astype(BF16)
    r_tt = r_t.T.astype(BF16)
    v_tt = v.T.astype(BF16)

    ri, ci, same = _chunk_masks(TP, C)
    up_strict = same & (ci > ri)
    up_incl = same & (ci >= ri)
    lane_head = _iota2((1, LANES), 1) >> 6
    heads = range(RW_H)
    units = [(p, h) for p in range(n_part) for h in heads]

    def hrows(h):
        return slice(h * RW_N, (h + 1) * RW_N)

    m_ba, m_br, m_k = {}, {}, {}
    for u in units:
        p, h = u
        tok = slice(p * TP, (p + 1) * TP)
        win = slice((h // 2) * LANES, (h // 2 + 1) * LANES)
        hmask = lane_head == (h % 2)
        lhs = jnp.concatenate([jnp.where(hmask, b_t[tok, win], 0.0),
                               jnp.where(hmask, k_t[tok, win], 0.0)], axis=0).astype(BF16)
        rhs = jnp.concatenate([a_tt[win, tok], r_tt[win, tok]], axis=1)
        m = _dot(lhs, rhs)
        m_ba[u] = jnp.where(up_strict, m[0:TP, 0:TP], 0.0)
        m_br[u] = jnp.where(up_incl, m[0:TP, TP:2 * TP], 0.0).astype(BF16)
        m_k[u] = jnp.concatenate([jnp.where(up_strict, m[TP:2 * TP, 0:TP], 0.0),
                                  jnp.where(up_incl, m[TP:2 * TP, TP:2 * TP], 0.0)], axis=1).astype(BF16)
    inv = _block_inverse_all([m_ba[u] for u in units], ri, ci, C)
    tts = {u: x.astype(BF16) for u, x in zip(units, inv)}
    g2 = {(p, h): _dot(v_tt[hrows(h), p * TP:(p + 1) * TP], m_k[(p, h)]) for p, h in units}
    wu = {(p, h): _dot(jnp.concatenate([a_tt[hrows(h), p * TP:(p + 1) * TP],
                                        g2[(p, h)][:, 0:TP].astype(BF16)], axis=0), tts[(p, h)])
          for p, h in units}
    w_t = {u: wu[u][0:RW_N, :].astype(BF16) for u in units}
    kv = [[_dot(v_tt[hrows(h), c * C:(c + 1) * C], k_h[c * C:(c + 1) * C, hrows(h)])
           for c in range(n_chunk)] for h in heads]

    st = [st_ref[hrows(h), :] for h in heads]
    yts = [[] for _ in heads]
    for c in range(n_chunk):
        c0 = c * C
        p = c // c_per_part
        cl = (c % c_per_part) * C
        d1 = [_dot(st[h].astype(BF16),
                   jnp.concatenate([w_t[(p, h)][:, cl:cl + C], r_tt[hrows(h), c0:c0 + C]], axis=1))
              for h in heads]
        ut = [(d1[h][:, 0:C] + wu[(p, h)][RW_N:2 * RW_N, cl:cl + C]).astype(BF16) for h in heads]
        d2 = [_dot(ut[h], jnp.concatenate([m_br[(p, h)][cl:cl + C, cl:cl + C],
                                            b_h[c0:c0 + C, hrows(h)]], axis=1))
              for h in heads]
        for h in heads:
            yts[h].append(d1[h][:, C:2 * C] + g2[(p, h)][:, TP + cl:TP + cl + C] + d2[h][:, 0:C])
            st[h] = st[h] * p_tot[c0:c0 + 1, hrows(h)] + d2[h][:, C:C + RW_N] + kv[h][c]
    yt_heads = []
    for h in heads:
        st_ref[hrows(h), :] = st[h]
        yt = jnp.concatenate(yts[h], axis=1)
        d = yt - jnp.mean(yt, axis=0, keepdims=True)
        yt_heads.append(d * lax.rsqrt(jnp.mean(d * d, axis=0, keepdims=True) + RW_GN_EPS))
    y = jnp.concatenate(yt_heads, axis=0).T * ln_g + ln_b
    bonus = _mask_dot_right(r * k2 * r_k, ones64) * v
    o_ref[...] = ((y + bonus) * _silu(z_ref[...])).astype(o_ref.dtype)


def _rwkv_call(p3, layer, mu, vec, up):
    _, bsz, seq, _ = p3.shape
    t = min(seq, MIX_T)
    kern = functools.partial(_rwkv_kernel, t_blk=t, chunk=min(RW_CHUNK, t), pair_t=min(RW_PAIR_T, t))
    return pl.pallas_call(
        kern,
        grid=(bsz, seq // t),
        in_specs=[_tile_spec(T_RW_R, t), _tile_spec(T_RW_K, t), _tile_spec(T_RW_V, t),
                  _tile_spec(T_SMALL, t), _tile_spec(T_RW_Z, t),
                  _layer_spec(layer, 8, TILE_W), _layer_spec(layer, 8, TILE_W),
                  _layer_spec(layer, LANES, 2 * TILE_W)],
        out_specs=pl.BlockSpec((None, t, BRANCH_W), lambda b, i: (b, i, 0)),
        out_shape=jax.ShapeDtypeStruct((bsz, seq, BRANCH_W), BF16),
        scratch_shapes=[pltpu.VMEM((8, TILE_W), F32), pltpu.VMEM((RW_H * RW_N, RW_N), F32)],
        compiler_params=_cparams(("parallel", "arbitrary")),
        name="rwkv7_mix",
    )(p3, p3, p3, p3, p3, mu, vec, up)


def _gla_recurrence(q, k, v, logg, st_ref, qs_ref, ks_ref, vs_ref, gs_ref, os_ref, *,
                    n_head, dk, t_blk, chunk):
    T, C = t_blk, chunk
    n_chunk = T // C
    dv = LANES

    def window(h):
        if dk == LANES:
            return h * LANES, None
        return (h // 2) * LANES, (_iota2((1, LANES), 1) >> 6) == (h % 2)

    ri, ci, same = _chunk_masks(T, C)
    tri = jnp.where(same & (ci <= ri), 1.0, 0.0).astype(BF16)
    blk = jnp.where(same, 1.0, 0.0).astype(BF16)
    b = _mask_dot_left(tri, logg)
    btot = _mask_dot_left(blk, logg)
    worst = jnp.max(jnp.max(-btot, axis=1, keepdims=True), axis=0, keepdims=True)
    fast = worst[0, 0] <= SAFE_EXP

    @pl.when(fast)
    def _():
        q_t = (q * jnp.exp(b)).astype(BF16)
        k_t = k * jnp.exp(-b)
        k_h = k * jnp.exp(btot - b)
        d_tot = jnp.exp(btot)
        v_tt = v.T.astype(BF16)
        vb = v.astype(BF16)
        causal = same & (ci <= ri)
        heads = range(n_head)
        chunks = range(n_chunk)
        qw, ktw, khw = [], [], []
        for h in heads:
            w0, hm = window(h)
            qw.append(q_t[:, w0:w0 + LANES])
            kt_h, kh_h = k_t[:, w0:w0 + LANES], k_h[:, w0:w0 + LANES]
            if hm is not None:
                kt_h, kh_h = jnp.where(hm, kt_h, 0.0), jnp.where(hm, kh_h, 0.0)
            ktw.append(kt_h.astype(BF16))
            khw.append(kh_h.astype(BF16))
        att = [jnp.where(causal, _dot_nt(qw[h], ktw[h]), 0.0).astype(BF16) for h in heads]
        ds = [[_dot(v_tt[h * dv:(h + 1) * dv, c * C:(c + 1) * C], khw[h][c * C:(c + 1) * C, :]) for c in chunks]
              for h in heads]
        sts = []
        for h in heads:
            w0, _ = window(h)
            st = st_ref[h * dv:(h + 1) * dv, :]
            row = []
            for c in chunks:
                row.append(st.astype(BF16))
                st = st * d_tot[c * C:c * C + 1, w0:w0 + LANES] + ds[h][c]
            st_ref[h * dv:(h + 1) * dv, :] = st
            sts.append(row)
        for h in heads:
            inter = [_dot_nt(qw[h][c * C:(c + 1) * C, :], sts[h][c]) for c in chunks]
            os_ref[:, h * dv:(h + 1) * dv] = (_dot(att[h], vb[:, h * dv:(h + 1) * dv])
                                              + jnp.concatenate(inter, axis=0))

    @pl.when(jnp.logical_not(fast))
    def _():
        qs_ref[...] = q
        ks_ref[...] = k
        gs_ref[...] = logg
        vs_ref[...] = v
        eye = _iota2((LANES, LANES), 0) == _iota2((LANES, LANES), 1)
        for h in range(n_head):
            w0, hm = window(h)

            def body(t8, st, w0=w0, hm=hm, h=h):
                rows = pl.ds(pl.multiple_of(t8 * 8, 8), 8)
                q8 = qs_ref[rows, w0:w0 + LANES]
                k8 = ks_ref[rows, w0:w0 + LANES]
                e8 = jnp.exp(gs_ref[rows, w0:w0 + LANES])
                v8 = vs_ref[rows, h * dv:(h + 1) * dv]
                if hm is not None:
                    k8 = jnp.where(hm, k8, 0.0)
                    q8 = jnp.where(hm, q8, 0.0)
                o_rows = []
                for j in range(8):
                    v_col = jnp.sum(jnp.where(eye, v8[j:j + 1, :], 0.0), axis=1, keepdims=True)
                    st = st * e8[j:j + 1, :] + v_col * k8[j:j + 1, :]
                    o_col = jnp.sum(st * q8[j:j + 1, :], axis=1, keepdims=True)
                    o_rows.append(jnp.sum(jnp.where(eye, o_col, 0.0), axis=0, keepdims=True))
                os_ref[rows, h * dv:(h + 1) * dv] = jnp.concatenate(o_rows, axis=0)
                return st

            st_ref[h * dv:(h + 1) * dv, :] = lax.fori_loop(0, T // 8, body, st_ref[h * dv:(h + 1) * dv, :])


def _head_rms_gate(o, g, z, n_head):
    outs = []
    for h in range(n_head):
        oh = o[:, h * LANES:(h + 1) * LANES]
        outs.append(oh * lax.rsqrt(jnp.mean(oh * oh, axis=-1, keepdims=True) + NORM_EPS))
    return jnp.concatenate(outs, axis=1) * g * _silu(z)


def _gla_kernel(qk_ref, v_ref, sm_ref, z_ref, up_ref, vec_ref, o_ref,
                st_ref, qs_ref, ks_ref, vs_ref, gs_ref, os_ref, *, t_blk, chunk):
    @pl.when(pl.program_id(1) == 0)
    def _():
        st_ref[...] = jnp.zeros_like(st_ref)

    hk = GLA_H * GLA_DK
    q = qk_ref[:, 0:hk] * (GLA_DK ** -0.5)
    k = qk_ref[:, hk:2 * hk]
    code = sm_ref[:, SM_GCODE:SM_GCODE + LANES].astype(BF16)
    logits = _dot(code, up_ref[...]) + vec_ref[0:1, 0:hk]
    logg = _log_sigmoid(logits) / GLA_LOGIT_NORM
    _gla_recurrence(q, k, v_ref[...], logg, st_ref, qs_ref, ks_ref, vs_ref, gs_ref, os_ref,
                    n_head=GLA_H, dk=GLA_DK, t_blk=t_blk, chunk=chunk)
    o_ref[...] = _head_rms_gate(os_ref[...], vec_ref[1:2, :], z_ref[...], GLA_H).astype(o_ref.dtype)


def _hgrn_kernel(q_ref, f_ref, i_ref, z_ref, vec_ref, o_ref,
                 st_ref, qs_ref, ks_ref, vs_ref, gs_ref, os_ref, *, t_blk, chunk):
    @pl.when(pl.program_id(1) == 0)
    def _():
        st_ref[...] = jnp.zeros_like(st_ref)

    lb = vec_ref[0:1, :]
    g = lb + (1.0 - lb) * _sigmoid(f_ref[...])
    _gla_recurrence(_silu(q_ref[...]), 1.0 - g, i_ref[...], jnp.log(g), st_ref,
                    qs_ref, ks_ref, vs_ref, gs_ref, os_ref,
                    n_head=HG_H, dk=LANES, t_blk=t_blk, chunk=chunk)
    o_ref[...] = _head_rms_gate(os_ref[...], vec_ref[1:2, :], z_ref[...], HG_H).astype(o_ref.dtype)


def _gla_scratch(t, key_w):
    return [pltpu.VMEM((4 * LANES, LANES), F32), pltpu.VMEM((t, key_w), F32), pltpu.VMEM((t, key_w), F32),
            pltpu.VMEM((t, BRANCH_W), F32), pltpu.VMEM((t, key_w), F32), pltpu.VMEM((t, BRANCH_W), F32)]


def _gla_call(p3, layer, up, vec):
    _, bsz, seq, _ = p3.shape
    t = min(seq, MIX_T)
    kern = functools.partial(_gla_kernel, t_blk=t, chunk=min(GLA_CHUNK, t))
    return pl.pallas_call(
        kern,
        grid=(bsz, seq // t),
        in_specs=[_tile_spec(T_GLA_QK, t), _tile_spec(T_GLA_V, t), _tile_spec(T_SMALL, t),
                  _tile_spec(T_GLA_Z, t), _layer_spec(layer, LANES, GLA_H * GLA_DK),
                  _layer_spec(layer, 8, TILE_W)],
        out_specs=pl.BlockSpec((None, t, BRANCH_W), lambda b, i: (b, i, 0)),
        out_shape=jax.ShapeDtypeStruct((bsz, seq, BRANCH_W), BF16),
        scratch_shapes=_gla_scratch(t, GLA_H * GLA_DK),
        compiler_params=_cparams(("parallel", "arbitrary")),
        name="gla_mix",
    )(p3, p3, p3, p3, up, vec)


def _hgrn_call(p3, layer, vec):
    _, bsz, seq, _ = p3.shape
    t = min(seq, MIX_T)
    chunk = HG_CHUNK_FIRST if layer == 0 else HG_CHUNK
    kern = functools.partial(_hgrn_kernel, t_blk=t, chunk=min(chunk, t))
    return pl.pallas_call(
        kern,
        grid=(bsz, seq // t),
        in_specs=[_tile_spec(T_HG_Q, t), _tile_spec(T_HG_F, t), _tile_spec(T_HG_I, t),
                  _tile_spec(T_HG_Z, t), _layer_spec(layer, 8, TILE_W)],
        out_specs=pl.BlockSpec((None, t, BRANCH_W), lambda b, i: (b, i, 0)),
        out_shape=jax.ShapeDtypeStruct((bsz, seq, BRANCH_W), BF16),
        scratch_shapes=_gla_scratch(t, TILE_W),
        compiler_params=_cparams(("parallel", "arbitrary")),
        name="hgrn2_mix",
    )(p3, p3, p3, p3, vec)


def _mlstm_kernel(qk_ref, v_ref, sm_ref, z_ref, conv_ref, vec_ref, o_ref,
                  prev_ref, ct_ref, n_ref, m_ref, *, t_blk, chunk):
    T, C = t_blk, chunk
    n_chunk = T // C
    heads = range(ML_H)

    @pl.when(pl.program_id(1) == 0)
    def _():
        prev_ref[...] = jnp.zeros_like(prev_ref)
        ct_ref[...] = jnp.zeros_like(ct_ref)
        n_ref[...] = jnp.zeros_like(n_ref)
        m_ref[...] = jnp.full_like(m_ref, STAB_INIT)

    x = qk_ref[...]
    xx = jnp.concatenate([prev_ref[...], x], axis=0)
    prev_ref[...] = x[T - 8:T, :]
    conv = x * conv_ref[ML_CONV - 1:ML_CONV, :]
    for d in range(1, ML_CONV):
        conv = conv + xx[8 - d:8 - d + T, :] * conv_ref[ML_CONV - 1 - d:ML_CONV - d, :]
    qk = _silu(conv)
    hq = ML_H * ML_DQK
    qb = qk[:, 0:hq].astype(BF16)
    kf = qk[:, hq:2 * hq] * (ML_DQK ** -0.5)
    kb = kf.astype(BF16)
    v = v_ref[...]
    vb = v.astype(BF16)
    v_tt = v.T.astype(BF16)

    gates = sm_ref[:, SM_GCODE:SM_GCODE + LANES]
    i_col = gates + vec_ref[0:1, 0:LANES]
    lf_col = _log_sigmoid(gates + vec_ref[1:2, 0:LANES])
    ri_t, ci_t, same_t = _chunk_masks(T, C)
    tri = jnp.where(same_t & (ci_t <= ri_t), 1.0, 0.0).astype(BF16)
    cf_col = _mask_dot_left(tri, lf_col)
    cf_row = cf_col.T
    i_row = i_col.T
    li, lf = SM_MI - SM_GCODE, SM_MF - SM_GCODE
    causal = _iota2((C, C), 1) <= _iota2((C, C), 0)
    lane_head = _iota2((1, LANES), 1) >> 6

    ct = [ct_ref[h * LANES:(h + 1) * LANES, :] for h in heads]
    n_row = [n_ref[h:h + 1, :] for h in heads]
    m_prev = [m_ref[h:h + 1, 0:1] for h in heads]
    outs = [[] for _ in heads]
    for c in range(n_chunk):
        tok = slice(c * C, (c + 1) * C)
        for h in heads:
            win = slice((h // 2) * LANES, (h // 2 + 1) * LANES)
            hm = lane_head == (h % 2)
            cf_i = cf_col[tok, lf + h:lf + h + 1]
            i_i = i_col[tok, li + h:li + h + 1]
            cf_j = cf_row[lf + h:lf + h + 1, tok]
            i_j = i_row[li + h:li + h + 1, tok]
            log_d = jnp.where(causal, cf_i - cf_j + i_j, -jnp.inf)
            log_inter = cf_i + m_prev[h]
            m_t = jnp.maximum(log_inter, jnp.max(log_d, axis=-1, keepdims=True))
            qh = jnp.where(hm, qb[tok, win], 0.0)
            s = _dot_nt(qh, kb[tok, win]) * jnp.exp(log_d - m_t)
            w_inter = jnp.exp(log_inter - m_t)
            num = (_dot(s.astype(BF16), vb[tok, h * LANES:(h + 1) * LANES])
                   + w_inter * _dot_nt(qh, ct[h].astype(BF16)))
            den = jnp.sum(s, axis=-1, keepdims=True) + w_inter * jnp.sum(
                jnp.where(hm, qk[tok, win], 0.0) * n_row[h], axis=-1, keepdims=True)
            outs[h].append(num / jnp.maximum(jnp.abs(den), jnp.exp(-m_t)))
            m_new = m_t[C - 1:C, :]
            cf_last = cf_i[C - 1:C, :]
            w_carry = jnp.exp(cf_last + m_prev[h] - m_new)
            kw = jnp.where(hm, kf[tok, win], 0.0) * jnp.exp(cf_last - cf_i + i_i - m_new)
            ct[h] = w_carry * ct[h] + _dot(v_tt[h * LANES:(h + 1) * LANES, tok], kw.astype(BF16))
            n_row[h] = w_carry * n_row[h] + jnp.sum(kw, axis=0, keepdims=True)
            m_prev[h] = m_new

    ln = []
    for h in heads:
        ct_ref[h * LANES:(h + 1) * LANES, :] = ct[h]
        n_ref[h:h + 1, :] = n_row[h]
        m_ref[h:h + 1, :] = jnp.broadcast_to(m_prev[h], (1, LANES))
        oh = jnp.concatenate(outs[h], axis=0)
        d = oh - jnp.mean(oh, axis=-1, keepdims=True)
        ln.append(d * lax.rsqrt(jnp.mean(d * d, axis=-1, keepdims=True) + NORM_EPS))
    o_ref[...] = (jnp.concatenate(ln, axis=1) * vec_ref[2:3, :] * _silu(z_ref[...])).astype(o_ref.dtype)


def _mlstm_call(p3, layer, conv_w, vec):
    _, bsz, seq, _ = p3.shape
    t = min(seq, MIX_T)
    kern = functools.partial(_mlstm_kernel, t_blk=t, chunk=min(ML_CHUNK, t))
    return pl.pallas_call(
        kern,
        grid=(bsz, seq // t),
        in_specs=[_tile_spec(T_ML_QK, t), _tile_spec(T_ML_V, t), _tile_spec(T_SMALL, t),
                  _tile_spec(T_ML_Z, t), _layer_spec(layer, 8, TILE_W), _layer_spec(layer, 8, TILE_W)],
        out_specs=pl.BlockSpec((None, t, BRANCH_W), lambda b, i: (b, i, 0)),
        out_shape=jax.ShapeDtypeStruct((bsz, seq, BRANCH_W), BF16),
        scratch_shapes=[pltpu.VMEM((8, TILE_W), F32), pltpu.VMEM((ML_H * LANES, LANES), F32),
                        pltpu.VMEM((8, LANES), F32), pltpu.VMEM((8, LANES), F32)],
        compiler_params=_cparams(("parallel", "arbitrary")),
        name="mlstm_mix",
    )(p3, p3, p3, p3, conv_w, vec)


def _merge_kernel(x_ref, ya_ref, yb_ref, yc_ref, yd_ref, *rest, final):
    gate_refs = rest[0:N_GATE_TILES]
    wb_ref, wo_ref, res_gate_ref, ng_ref, nsc_ref, nsh_ref = rest[N_GATE_TILES:N_GATE_TILES + 6]
    out_refs = rest[N_GATE_TILES + 6:]
    per_branch = N_GATE_TILES // N_BRANCH
    merged = None
    for m, y_ref in enumerate((ya_ref, yb_ref, yc_ref, yd_ref)):
        br = _dot(y_ref[...], wb_ref[m])
        gates = jnp.concatenate([gate_refs[per_branch * m + j][...] for j in range(per_branch)], axis=1)
        term = gates * br
        merged = term if merged is None else merged + term
    x_new = x_ref[...] + res_gate_ref[...] * _dot(merged.astype(BF16), wo_ref[...])
    ms = jnp.mean(x_new * x_new, axis=-1, keepdims=True)
    normed = x_new * lax.rsqrt(ms + NORM_EPS) * ng_ref[...]
    if final:
        out_refs[0][...] = normed
    else:
        out_refs[0][...] = x_new
        out_refs[1][...] = (normed * (1.0 + nsc_ref[...]) + nsh_ref[...]).astype(BF16)


def _merge_call(x, ys, gates, layer, w_branch, w_out, res_gate, next_g, next_scale, next_shift, final):
    bsz, seq, d = x.shape
    tm = min(seq, 512)
    row = pl.BlockSpec((None, tm, d), lambda b, i: (b, i, 0))
    y_spec = pl.BlockSpec((None, tm, BRANCH_W), lambda b, i: (b, i, 0))
    vec = pl.BlockSpec((None, 1, d), lambda b, i: (b, 0, 0))
    gate_specs = [_tile_spec(j, tm) for j in range(N_GATE_TILES)]
    out_shape = [jax.ShapeDtypeStruct((bsz, seq, d), F32)]
    out_specs = [row]
    if not final:
        out_shape.append(jax.ShapeDtypeStruct((bsz, seq, d), BF16))
        out_specs.append(row)
    return pl.pallas_call(
        functools.partial(_merge_kernel, final=final),
        grid=(bsz, seq // tm),
        in_specs=[row, y_spec, y_spec, y_spec, y_spec] + gate_specs
        + [pl.BlockSpec((None, N_BRANCH, BRANCH_W, d), lambda b, i: (layer, 0, 0, 0)),
           pl.BlockSpec((None, d, d), lambda b, i: (layer, 0, 0)),
           vec, pl.BlockSpec((1, d), lambda b, i: (0, 0)), vec, vec],
        out_specs=out_specs,
        out_shape=out_shape,
        compiler_params=_cparams(("parallel", "parallel")),
        name="merge_out",
    )(x, *ys, *([gates] * N_GATE_TILES), w_branch, w_out, res_gate.reshape(bsz, 1, d),
      next_g.reshape(1, d), next_scale.reshape(bsz, 1, d), next_shift.reshape(bsz, 1, d))


def _pad_last(a, width):
    return jnp.pad(a, [(0, 0)] * (a.ndim - 1) + [(0, width - a.shape[-1])])


def _rows(rows, width, n_rows=8):
    out = [_pad_last(r.astype(F32), width)[:, None, :] for r in rows]
    out.append(jnp.zeros((rows[0].shape[0], n_rows - len(rows), width), F32))
    return jnp.concatenate(out, axis=1)


def _relayout_w_in(w):
    o = 0

    def take(n):
        nonlocal o
        s = w[:, :, o:o + n]
        o += n
        return s

    rw_r, rw_k, rw_v, rw_wc, rw_ac, rw_z = take(512), take(512), take(512), take(64), take(64), take(512)
    gq, gk, gv, gcode, gz = take(256), take(256), take(512), take(16), take(512)
    mqk, mv, mi, mf, mz = take(512), take(512), take(4), take(4), take(512)
    hq, hf, hi, hz = take(512), take(512), take(512), take(512)
    merge = take(N_BRANCH * D_MODEL)
    small = _pad_last(jnp.concatenate([rw_wc, rw_ac, gcode, mi, mf], axis=2), TILE_W)
    tiles = [rw_r, rw_k, rw_v, rw_z, small, jnp.concatenate([gq, gk], axis=2), gv, gz,
             mqk, mv, mz, hq, hf, hi, hz]
    gate_tiles = [merge[:, :, j * TILE_W:(j + 1) * TILE_W] for j in range(N_GATE_TILES)]
    return jnp.stack(tiles, axis=1).astype(BF16), jnp.stack(gate_tiles, axis=1).astype(BF16)


def kernel(x, c, norm_g, ada_w, ada_b, w_in, rw_mu, rw_w0, rw_w_up, rw_a0, rw_a_up, rw_k_k, rw_k_a,
           rw_r_k, rw_ln_g, rw_ln_b, gla_gk_up, gla_gk_b, gla_norm_g, ml_conv_w, ml_i_b, ml_f_b,
           ml_norm_g, hg_lb_logits, hg_norm_g, w_branch, w_out, final_g):
    depth = w_in.shape[0]
    d = x.shape[-1]
    lb_p = jax.nn.softmax(hg_lb_logits.astype(F32), axis=0)
    lower_bounds = jnp.cumsum(lb_p, axis=0) - lb_p[0]

    w_mix, w_gate = _relayout_w_in(w_in)
    rw_mu_rows = _rows([rw_mu[:, 0:512], rw_mu[:, 512:1024], rw_mu[:, 1024:1536], rw_mu[:, 1536:1664]], TILE_W)
    rw_vec = _rows([rw_w0, rw_a0, rw_k_k, rw_k_a, rw_r_k, rw_ln_g, rw_ln_b], TILE_W)
    zeros = jnp.zeros((depth, RW_RANK, BRANCH_W), F32)
    rw_up = jnp.concatenate([jnp.concatenate([rw_w_up, zeros], axis=2),
                             jnp.concatenate([zeros, rw_a_up], axis=2)], axis=1).astype(BF16)
    gla_up = jnp.pad(gla_gk_up, ((0, 0), (0, LANES - GLA_RANK), (0, 0))).astype(BF16)
    gla_vec = _rows([gla_gk_b, gla_norm_g], TILE_W)
    ml_conv = jnp.pad(ml_conv_w.astype(F32), ((0, 0), (0, 8 - ML_CONV), (0, 0)))
    ml_vec = _rows([jnp.pad(ml_i_b, ((0, 0), (SM_MI - SM_GCODE, 0))),
                    jnp.pad(ml_f_b, ((0, 0), (SM_MF - SM_GCODE, 0))), ml_norm_g], TILE_W)
    hg_vec = _rows([lower_bounds, hg_norm_g], TILE_W)
    wb = w_branch.astype(BF16)
    wo = w_out.astype(BF16)

    mod = _ada_call(c, ada_w, ada_b)
    u = _norm_call(x, norm_g[0], mod[0, :, d:2 * d], mod[0, :, 0:d])
    for l in range(depth):
        p3 = _proj_call(u, w_mix, l, group=3, gate=False, name="in_proj")
        gates = _proj_call(u, w_gate, l, group=4, gate=True, name="gate_proj")
        y_a = _rwkv_call(p3, l, rw_mu_rows, rw_vec, rw_up)
        y_b = _gla_call(p3, l, gla_up, gla_vec)
        y_c = _mlstm_call(p3, l, ml_conv, ml_vec)
        y_d = _hgrn_call(p3, l, hg_vec)
        final = l == depth - 1
        if final:
            next_g, next_scale, next_shift = final_g, mod[l, :, d:2 * d], mod[l, :, 0:d]
        else:
            next_g, next_scale, next_shift = norm_g[l + 1], mod[l + 1, :, d:2 * d], mod[l + 1, :, 0:d]
        res = _merge_call(x, (y_a, y_b, y_c, y_d), gates, l, wb, wo, mod[l, :, 2 * d:3 * d],
                          next_g, next_scale, next_shift, final)
        if final:
            x = res[0]
        else:
            x, u = res
    return x
```

```python
import functools

import jax
import jax.numpy as jnp
import numpy as np
from jax import lax
from jax.experimental import pallas as pl
from jax.experimental.pallas import tpu as pltpu

F32 = jnp.float32
BF16 = jnp.bfloat16

D_MODEL = 1024
N_BRANCH = 4
BRANCH_W = 512
RW_N = 64
RW_H = 8
RW_RANK = 64
RW_GN_EPS = 64e-5
GLA_H = 4
GLA_DK = 64
GLA_RANK = 16
GLA_LOGIT_NORM = 16.0
ML_H = 4
ML_DQK = 64
ML_CONV = 4
STAB_INIT = -1e30
HG_H = 4
NORM_EPS = 1e-6

LANES = 128
TILE_W = 512
MIX_T = 256
MIX_SUB = 2
RW_CHUNK = 64
RW_PAIR_T = 128
GLA_CHUNK = 128
HG_CHUNK_FIRST = 16
HG_CHUNK = 32
ML_CHUNK = 256
SAFE_EXP = 80.0
VMEM_LIMIT = 56 * 1024 * 1024

T_RW_R, T_RW_K, T_RW_V, T_RW_Z, T_SMALL = 0, 1, 2, 3, 4
T_GLA_QK, T_GLA_V, T_GLA_Z = 5, 6, 7
T_ML_QK, T_ML_V, T_ML_Z = 8, 9, 10
T_HG_Q, T_HG_F, T_HG_I, T_HG_Z = 11, 12, 13, 14
N_MIX_TILES = 15
N_GATE_TILES = N_BRANCH * D_MODEL // TILE_W
SM_GCODE = 128
SM_MI = 144
SM_MF = 148


def _cparams(sem):
    return pltpu.CompilerParams(dimension_semantics=sem, vmem_limit_bytes=VMEM_LIMIT)


def _sigmoid(x):
    return 1.0 / (1.0 + jnp.exp(-x))


def _silu(x):
    return x * _sigmoid(x)


def _softplus(x):
    return jnp.maximum(x, 0.0) + jnp.log(1.0 + jnp.exp(-jnp.abs(x)))


def _log_sigmoid(x):
    return -_softplus(-x)


def _split2(x):
    hi = x.astype(BF16)
    lo = (x - hi.astype(F32)).astype(BF16)
    return hi, lo


def _dot(a, b):
    return jnp.dot(a, b, preferred_element_type=F32)


def _dot_nt(a, b):
    return lax.dot_general(a, b, (((1,), (1,)), ((), ())), preferred_element_type=F32)


def _mask_dot_left(m01, x):
    hi, lo = _split2(x)
    return _dot(m01, hi) + _dot(m01, lo)


def _iota2(shape, dim):
    return lax.broadcasted_iota(jnp.int32, shape, dim)


def _chunk_masks(t, c):
    sh = int(np.log2(c))
    ri = _iota2((t, t), 0)
    ci = _iota2((t, t), 1)
    same = (ri >> sh) == (ci >> sh)
    return ri, ci, same


def _seg_ones(width, seg):
    sh = int(np.log2(seg))
    ri = _iota2((width, width), 0)
    ci = _iota2((width, width), 1)
    return jnp.where((ri >> sh) == (ci >> sh), 1.0, 0.0).astype(BF16)


def _rms_mod(x, g, scale, shift):
    ms = jnp.mean(x * x, axis=-1, keepdims=True)
    return x * lax.rsqrt(ms + NORM_EPS) * g * (1.0 + scale) + shift


def _ada_kernel(c_ref, w_ref, b_ref, o_ref):
    cond = _silu(c_ref[...]).astype(BF16)
    o_ref[...] = _dot(cond, w_ref[...].astype(BF16)) + b_ref[...]


def _ada_call(c, ada_w, ada_b):
    depth, d, n3 = ada_w.shape
    bsz = c.shape[0]
    tn = 1024
    return pl.pallas_call(
        _ada_kernel,
        grid=(depth, n3 // tn),
        in_specs=[
            pl.BlockSpec((bsz, d), lambda l, j: (0, 0)),
            pl.BlockSpec((None, d, tn), lambda l, j: (l, 0, j)),
            pl.BlockSpec((None, 1, tn), lambda l, j: (l, 0, j)),
        ],
        out_specs=pl.BlockSpec((None, bsz, tn), lambda l, j: (l, 0, j)),
        out_shape=jax.ShapeDtypeStruct((depth, bsz, n3), F32),
        compiler_params=_cparams(("parallel", "parallel")),
        name="ada_mod",
    )(c, ada_w, ada_b.reshape(depth, 1, n3))


def _norm_kernel(x_ref, g_ref, sc_ref, sh_ref, u_ref):
    u_ref[...] = _rms_mod(x_ref[...], g_ref[...], sc_ref[...], sh_ref[...]).astype(u_ref.dtype)


def _norm_call(x, g, scale, shift):
    bsz, seq, d = x.shape
    tm = min(seq, 1024)
    return pl.pallas_call(
        _norm_kernel,
        grid=(bsz, seq // tm),
        in_specs=[pl.BlockSpec((None, tm, d), lambda b, i: (b, i, 0)),
                  pl.BlockSpec((1, d), lambda b, i: (0, 0)),
                  pl.BlockSpec((None, 1, d), lambda b, i: (b, 0, 0)),
                  pl.BlockSpec((None, 1, d), lambda b, i: (b, 0, 0))],
        out_specs=pl.BlockSpec((None, tm, d), lambda b, i: (b, i, 0)),
        out_shape=jax.ShapeDtypeStruct((bsz, seq, d), BF16),
        compiler_params=_cparams(("parallel", "parallel")),
        name="norm_mod",
    )(x, g.reshape(1, d), scale.reshape(bsz, 1, d), shift.reshape(bsz, 1, d))


def _proj_kernel(u_ref, w_ref, o_ref, *, group, gate):
    u = u_ref[...]
    for t in range(group):
        r = _dot(u, w_ref[:, t * TILE_W:(t + 1) * TILE_W])
        if gate:
            r = _sigmoid(r)
        o_ref[t] = r.astype(o_ref.dtype)


def _proj_call(u, w_tiles, layer, *, group, gate, name):
    bsz, seq, d = u.shape
    n_tiles = w_tiles.shape[2] // TILE_W
    tm = min(seq, 2048)
    return pl.pallas_call(
        functools.partial(_proj_kernel, group=group, gate=gate),
        grid=(bsz, seq // tm, n_tiles // group),
        in_specs=[pl.BlockSpec((None, tm, d), lambda b, i, j: (b, i, 0)),
                  pl.BlockSpec((None, d, group * TILE_W), lambda b, i, j: (layer, 0, j))],
        out_specs=pl.BlockSpec((group, None, tm, TILE_W), lambda b, i, j: (j, b, i, 0)),
        out_shape=jax.ShapeDtypeStruct((n_tiles, bsz, seq, TILE_W), BF16 if gate else F32),
        compiler_params=_cparams(("parallel", "parallel", "arbitrary")),
        name=name,
    )(u, w_tiles)


def _mix_tiling(seq):
    sub_t = min(seq, MIX_T)
    n_sub = MIX_SUB if seq % (sub_t * MIX_SUB) == 0 else 1
    return sub_t * n_sub, n_sub


def _sub(ref, s, t):
    return ref.at[pl.ds(s * t, t), :]


def _tile_spec(tile_id, t):
    return pl.BlockSpec((None, None, t, TILE_W), lambda b, i: (tile_id, b, i, 0))


def _layer_spec(layer, rows, width):
    return pl.BlockSpec((None, rows, width), lambda b, i: (layer, 0, 0))


def _block_inverse_all(n0s, ri, ci, c):
    eye = jnp.where(ri == ci, 1.0, 0.0)
    lvl1 = (ri >> 1) == (ci >> 1)
    xs = [eye + jnp.where(lvl1, n0, 0.0) for n0 in n0s]
    for lv in range(2, int(np.log2(c)) + 1):
        m = ((ri >> lv) == (ci >> lv)) & ((ri >> (lv - 1)) != (ci >> (lv - 1)))
        xbs = [x.astype(BF16) for x in xs]
        xns = [_dot(xb, jnp.where(m, n0, 0.0).astype(BF16)).astype(BF16) for xb, n0 in zip(xbs, n0s)]
        xs = [x + _dot(xn, xb) for x, xn, xb in zip(xs, xns, xbs)]
    return xs


def _rwkv_kernel(r_ref, k_ref, v_ref, sm_ref, z_ref, mu_ref, vec_ref, up_ref, o_ref,
                 carry_ref, st_ref, *, t_blk, n_sub, chunk, pair_t):
    @pl.when(pl.program_id(1) == 0)
    def _():
        carry_ref[...] = jnp.zeros_like(carry_ref)
        st_ref[...] = jnp.zeros_like(st_ref)

    for s in range(n_sub):
        _rwkv_tile(*[_sub(ref, s, t_blk) for ref in (r_ref, k_ref, v_ref, sm_ref, z_ref)],
                   mu_ref, vec_ref, up_ref, _sub(o_ref, s, t_blk), carry_ref, st_ref,
                   t_blk=t_blk, chunk=chunk, pair_t=pair_t)


def _rwkv_tile(r_ref, k_ref, v_ref, sm_ref, z_ref, mu_ref, vec_ref, up_ref, o_ref,
               carry_ref, st_ref, *, t_blk, chunk, pair_t):
    T, C, TP = t_blk, chunk, pair_t
    n_chunk = T // C
    n_part = T // TP
    c_per_part = TP // C

    row0 = _iota2((T, 1), 0) == 0

    def shift_mix(x, slot, width, mu):
        prev = carry_ref[slot:slot + 1, 0:width]
        xp = jnp.where(row0, prev, pltpu.roll(x, 1, 0))
        carry_ref[slot:slot + 1, 0:width] = x[T - 1:T, :]
        return x + (xp - x) * mu

    r = shift_mix(r_ref[...], 0, TILE_W, mu_ref[0:1, :])
    k = shift_mix(k_ref[...], 1, TILE_W, mu_ref[1:2, :])
    v = shift_mix(v_ref[...], 2, TILE_W, mu_ref[2:3, :])
    codes = shift_mix(sm_ref[:, 0:LANES], 3, LANES, mu_ref[3:4, 0:LANES])

    w0, a0 = vec_ref[0:1, :], vec_ref[1:2, :]
    k_k, k_a, r_k = vec_ref[2:3, :], vec_ref[3:4, :], vec_ref[4:5, :]
    ln_g, ln_b = vec_ref[5:6, :], vec_ref[6:7, :]

    lane = _iota2((T, LANES), 1)
    codes_t = jnp.where(lane < RW_RANK, jnp.tanh(codes), codes).astype(BF16)
    proj = _dot(codes_t, up_ref[...])
    w = -_softplus(-(w0 + proj[:, 0:TILE_W])) - 0.5
    lw = -jnp.exp(w)
    a = _sigmoid(a0 + proj[:, TILE_W:2 * TILE_W])

    ones64 = _seg_ones(TILE_W, RW_N)
    kkr = k * k_k
    ss = _dot((kkr * kkr).astype(BF16), ones64)
    kk = kkr / jnp.maximum(jnp.sqrt(ss), 1e-12)
    k2 = k * (1.0 + (a - 1.0) * k_a)
    alpha = -kk
    beta = kk * a

    ri_t, ci_t, same_t = _chunk_masks(T, C)
    tri = jnp.where(same_t & (ci_t <= ri_t), 1.0, 0.0).astype(BF16)
    blk = jnp.where(same_t, 1.0, 0.0).astype(BF16)
    cw = _mask_dot_left(tri, lw)
    cwl = _mask_dot_left(blk, lw)
    e_in = jnp.exp(cw)
    e_out = jnp.exp(-cw)
    e_end = jnp.exp(cwl - cw)
    a_t = alpha * jnp.exp(cw - lw)
    r_t = r * e_in
    b_t = beta * e_out
    k_t = k2 * e_out
    b_h = (beta * e_end).astype(BF16)
    k_h = (k2 * e_end).astype(BF16)
    p_tot = jnp.exp(cwl)

    a_tt = a_t.T.astype(BF16)
    r_tt = r_t.T.astype(BF16)
    v_tt = v.T.astype(BF16)

    ri, ci, same = _chunk_masks(TP, C)
    up_strict = same & (ci > ri)
    up_incl = same & (ci >= ri)
    lane_head = _iota2((1, LANES), 1) >> 6
    heads = range(RW_H)
    units = [(p, h) for p in range(n_part) for h in heads]

    def hrows(h):
        return slice(h * RW_N, (h + 1) * RW_N)

    m_ba, m_br, m_k = {}, {}, {}
    for u in units:
        p, h = u
        tok = slice(p * TP, (p + 1) * TP)
        win = slice((h // 2) * LANES, (h // 2 + 1) * LANES)
        hmask = lane_head == (h % 2)
        lhs = jnp.concatenate([jnp.where(hmask, b_t[tok, win], 0.0),
                               jnp.where(hmask, k_t[tok, win], 0.0)], axis=0).astype(BF16)
        rhs = jnp.concatenate([a_tt[win, tok], r_tt[win, tok]], axis=1)
        m = _dot(lhs, rhs)
        m_ba[u] = jnp.where(up_strict, m[0:TP, 0:TP], 0.0)
        m_br[u] = jnp.where(up_incl, m[0:TP, TP:2 * TP], 0.0).astype(BF16)
        m_k[u] = jnp.concatenate([jnp.where(up_strict, m[TP:2 * TP, 0:TP], 0.0),
                                  jnp.where(up_incl, m[TP:2 * TP, TP:2 * TP], 0.0)], axis=1).astype(BF16)
    inv = _block_inverse_all([m_ba[u] for u in units], ri, ci, C)
    tts = {u: x.astype(BF16) for u, x in zip(units, inv)}
    g2 = {(p, h): _dot(v_tt[hrows(h), p * TP:(p + 1) * TP], m_k[(p, h)]) for p, h in units}
    wu = {(p, h): _dot(jnp.concatenate([a_tt[hrows(h), p * TP:(p + 1) * TP],
                                        g2[(p, h)][:, 0:TP].astype(BF16)], axis=0), tts[(p, h)])
          for p, h in units}
    w_t = {u: wu[u][0:RW_N, :].astype(BF16) for u in units}
    kv = [[_dot(v_tt[hrows(h), c * C:(c + 1) * C], k_h[c * C:(c + 1) * C, hrows(h)])
           for c in range(n_chunk)] for h in heads]

    st = [st_ref[hrows(h), :] for h in heads]
    yts = [[] for _ in heads]
    for c in range(n_chunk):
        c0 = c * C
        p = c // c_per_part
        cl = (c % c_per_part) * C
        d1 = [_dot(st[h].astype(BF16),
                   jnp.concatenate([w_t[(p, h)][:, cl:cl + C], r_tt[hrows(h), c0:c0 + C]], axis=1))
              for h in heads]
        ut = [(d1[h][:, 0:C] + wu[(p, h)][RW_N:2 * RW_N, cl:cl + C]).astype(BF16) for h in heads]
        d2 = [_dot(ut[h], jnp.concatenate([m_br[(p, h)][cl:cl + C, cl:cl + C],
                                            b_h[c0:c0 + C, hrows(h)]], axis=1))
              for h in heads]
        for h in heads:
            yts[h].append(d1[h][:, C:2 * C] + g2[(p, h)][:, TP + cl:TP + cl + C] + d2[h][:, 0:C])
            st[h] = st[h] * p_tot[c0:c0 + 1, hrows(h)] + d2[h][:, C:C + RW_N] + kv[h][c]
    yt_heads = []
    for h in heads:
        st_ref[hrows(h), :] = st[h]
        yt = jnp.concatenate(yts[h], axis=1)
        d = yt - jnp.mean(yt, axis=0, keepdims=True)
        yt_heads.append(d * lax.rsqrt(jnp.mean(d * d, axis=0, keepdims=True) + RW_GN_EPS))
    y = jnp.concatenate(yt_heads, axis=0).T * ln_g + ln_b
    bonus = _dot((r * k2 * r_k).astype(BF16), ones64) * v
    o_ref[...] = ((y + bonus) * _silu(z_ref[...])).astype(o_ref.dtype)


def _rwkv_call(p3, layer, mu, vec, up):
    _, bsz, seq, _ = p3.shape
    t, n_sub = _mix_tiling(seq)
    kern = functools.partial(_rwkv_kernel, t_blk=t // n_sub, n_sub=n_sub, chunk=min(RW_CHUNK, t),
                             pair_t=min(RW_PAIR_T, t))
    return pl.pallas_call(
        kern,
        grid=(bsz, seq // t),
        in_specs=[_tile_spec(T_RW_R, t), _tile_spec(T_RW_K, t), _tile_spec(T_RW_V, t),
                  _tile_spec(T_SMALL, t), _tile_spec(T_RW_Z, t),
                  _layer_spec(layer, 8, TILE_W), _layer_spec(layer, 8, TILE_W),
                  _layer_spec(layer, LANES, 2 * TILE_W)],
        out_specs=pl.BlockSpec((None, t, BRANCH_W), lambda b, i: (b, i, 0)),
        out_shape=jax.ShapeDtypeStruct((bsz, seq, BRANCH_W), BF16),
        scratch_shapes=[pltpu.VMEM((8, TILE_W), F32), pltpu.VMEM((RW_H * RW_N, RW_N), F32)],
        compiler_params=_cparams(("parallel", "arbitrary")),
        name="rwkv7_mix",
    )(p3, p3, p3, p3, p3, mu, vec, up)


def _gla_recurrence(q, k, v, logg, st_ref, qs_ref, ks_ref, vs_ref, gs_ref, os_ref, *,
                    n_head, dk, t_blk, chunk):
    T, C = t_blk, chunk
    n_chunk = T // C
    dv = LANES

    def window(h):
        if dk == LANES:
            return h * LANES, None
        return (h // 2) * LANES, (_iota2((1, LANES), 1) >> 6) == (h % 2)

    ri, ci, same = _chunk_masks(T, C)
    tri = jnp.where(same & (ci <= ri), 1.0, 0.0).astype(BF16)
    blk = jnp.where(same, 1.0, 0.0).astype(BF16)
    b = _mask_dot_left(tri, logg)
    btot = _mask_dot_left(blk, logg)
    worst = jnp.max(jnp.max(-btot, axis=1, keepdims=True), axis=0, keepdims=True)
    fast = worst[0, 0] <= SAFE_EXP

    @pl.when(fast)
    def _():
        q_t = (q * jnp.exp(b)).astype(BF16)
        k_t = k * jnp.exp(-b)
        k_h = k * jnp.exp(btot - b)
        d_tot = jnp.exp(btot)
        v_tt = v.T.astype(BF16)
        vb = v.astype(BF16)
        causal = same & (ci <= ri)
        heads = range(n_head)
        chunks = range(n_chunk)
        qw, ktw, khw = [], [], []
        for h in heads:
            w0, hm = window(h)
            qw.append(q_t[:, w0:w0 + LANES])
            kt_h, kh_h = k_t[:, w0:w0 + LANES], k_h[:, w0:w0 + LANES]
            if hm is not None:
                kt_h, kh_h = jnp.where(hm, kt_h, 0.0), jnp.where(hm, kh_h, 0.0)
            ktw.append(kt_h.astype(BF16))
            khw.append(kh_h.astype(BF16))
        att = [jnp.where(causal, _dot_nt(qw[h], ktw[h]), 0.0).astype(BF16) for h in heads]
        ds = [[_dot(v_tt[h * dv:(h + 1) * dv, c * C:(c + 1) * C], khw[h][c * C:(c + 1) * C, :]) for c in chunks]
              for h in heads]
        sts = []
        for h in heads:
            w0, _ = window(h)
            st = st_ref[h * dv:(h + 1) * dv, :]
            row = []
            for c in chunks:
                row.append(st.astype(BF16))
                st = st * d_tot[c * C:c * C + 1, w0:w0 + LANES] + ds[h][c]
            st_ref[h * dv:(h + 1) * dv, :] = st
            sts.append(row)
        for h in heads:
            inter = [_dot_nt(qw[h][c * C:(c + 1) * C, :], sts[h][c]) for c in chunks]
            os_ref[:, h * dv:(h + 1) * dv] = (_dot(att[h], vb[:, h * dv:(h + 1) * dv])
                                              + jnp.concatenate(inter, axis=0))

    @pl.when(jnp.logical_not(fast))
    def _():
        qs_ref[...] = q
        ks_ref[...] = k
        gs_ref[...] = logg
        vs_ref[...] = v
        eye = _iota2((LANES, LANES), 0) == _iota2((LANES, LANES), 1)
        for h in range(n_head):
            w0, hm = window(h)

            def body(t8, st, w0=w0, hm=hm, h=h):
                rows = pl.ds(pl.multiple_of(t8 * 8, 8), 8)
                q8 = qs_ref[rows, w0:w0 + LANES]
                k8 = ks_ref[rows, w0:w0 + LANES]
                e8 = jnp.exp(gs_ref[rows, w0:w0 + LANES])
                v8 = vs_ref[rows, h * dv:(h + 1) * dv]
                if hm is not None:
                    k8 = jnp.where(hm, k8, 0.0)
                    q8 = jnp.where(hm, q8, 0.0)
                o_rows = []
                for j in range(8):
                    v_col = jnp.sum(jnp.where(eye, v8[j:j + 1, :], 0.0), axis=1, keepdims=True)
                    st = st * e8[j:j + 1, :] + v_col * k8[j:j + 1, :]
                    o_col = jnp.sum(st * q8[j:j + 1, :], axis=1, keepdims=True)
                    o_rows.append(jnp.sum(jnp.where(eye, o_col, 0.0), axis=0, keepdims=True))
                os_ref[rows, h * dv:(h + 1) * dv] = jnp.concatenate(o_rows, axis=0)
                return st

            st_ref[h * dv:(h + 1) * dv, :] = lax.fori_loop(0, T // 8, body, st_ref[h * dv:(h + 1) * dv, :])


def _head_rms_gate(o, g, z, n_head):
    outs = []
    for h in range(n_head):
        oh = o[:, h * LANES:(h + 1) * LANES]
        outs.append(oh * lax.rsqrt(jnp.mean(oh * oh, axis=-1, keepdims=True) + NORM_EPS))
    return jnp.concatenate(outs, axis=1) * g * _silu(z)


def _gla_kernel(qk_ref, v_ref, sm_ref, z_ref, up_ref, vec_ref, o_ref, st_ref, *scratch, t_blk, n_sub, chunk):
    @pl.when(pl.program_id(1) == 0)
    def _():
        st_ref[...] = jnp.zeros_like(st_ref)

    for s in range(n_sub):
        _gla_tile(*[_sub(ref, s, t_blk) for ref in (qk_ref, v_ref, sm_ref, z_ref)], up_ref, vec_ref,
                  _sub(o_ref, s, t_blk), st_ref, *scratch, t_blk=t_blk, chunk=chunk)


def _gla_tile(qk_ref, v_ref, sm_ref, z_ref, up_ref, vec_ref, o_ref,
              st_ref, qs_ref, ks_ref, vs_ref, gs_ref, os_ref, *, t_blk, chunk):
    hk = GLA_H * GLA_DK
    q = qk_ref[:, 0:hk] * (GLA_DK ** -0.5)
    k = qk_ref[:, hk:2 * hk]
    code = sm_ref[:, SM_GCODE:SM_GCODE + LANES].astype(BF16)
    logits = _dot(code, up_ref[...]) + vec_ref[0:1, 0:hk]
    logg = _log_sigmoid(logits) / GLA_LOGIT_NORM
    _gla_recurrence(q, k, v_ref[...], logg, st_ref, qs_ref, ks_ref, vs_ref, gs_ref, os_ref,
                    n_head=GLA_H, dk=GLA_DK, t_blk=t_blk, chunk=chunk)
    o_ref[...] = _head_rms_gate(os_ref[...], vec_ref[1:2, :], z_ref[...], GLA_H).astype(o_ref.dtype)


def _hgrn_kernel(q_ref, f_ref, i_ref, z_ref, vec_ref, o_ref, st_ref, *scratch, t_blk, n_sub, chunk):
    @pl.when(pl.program_id(1) == 0)
    def _():
        st_ref[...] = jnp.zeros_like(st_ref)

    for s in range(n_sub):
        _hgrn_tile(*[_sub(ref, s, t_blk) for ref in (q_ref, f_ref, i_ref, z_ref)], vec_ref,
                   _sub(o_ref, s, t_blk), st_ref, *scratch, t_blk=t_blk, chunk=chunk)


def _hgrn_tile(q_ref, f_ref, i_ref, z_ref, vec_ref, o_ref,
               st_ref, qs_ref, ks_ref, vs_ref, gs_ref, os_ref, *, t_blk, chunk):
    lb = vec_ref[0:1, :]
    g = lb + (1.0 - lb) * _sigmoid(f_ref[...])
    _gla_recurrence(_silu(q_ref[...]), 1.0 - g, i_ref[...], jnp.log(g), st_ref,
                    qs_ref, ks_ref, vs_ref, gs_ref, os_ref,
                    n_head=HG_H, dk=LANES, t_blk=t_blk, chunk=chunk)
    o_ref[...] = _head_rms_gate(os_ref[...], vec_ref[1:2, :], z_ref[...], HG_H).astype(o_ref.dtype)


def _gla_scratch(t, key_w):
    return [pltpu.VMEM((4 * LANES, LANES), F32), pltpu.VMEM((t, key_w), F32), pltpu.VMEM((t, key_w), F32),
            pltpu.VMEM((t, BRANCH_W), F32), pltpu.VMEM((t, key_w), F32), pltpu.VMEM((t, BRANCH_W), F32)]


def _gla_call(p3, layer, up, vec):
    _, bsz, seq, _ = p3.shape
    t, n_sub = _mix_tiling(seq)
    kern = functools.partial(_gla_kernel, t_blk=t // n_sub, n_sub=n_sub, chunk=min(GLA_CHUNK, t))
    return pl.pallas_call(
        kern,
        grid=(bsz, seq // t),
        in_specs=[_tile_spec(T_GLA_QK, t), _tile_spec(T_GLA_V, t), _tile_spec(T_SMALL, t),
                  _tile_spec(T_GLA_Z, t), _layer_spec(layer, LANES, GLA_H * GLA_DK),
                  _layer_spec(layer, 8, TILE_W)],
        out_specs=pl.BlockSpec((None, t, BRANCH_W), lambda b, i: (b, i, 0)),
        out_shape=jax.ShapeDtypeStruct((bsz, seq, BRANCH_W), BF16),
        scratch_shapes=_gla_scratch(t // n_sub, GLA_H * GLA_DK),
        compiler_params=_cparams(("parallel", "arbitrary")),
        name="gla_mix",
    )(p3, p3, p3, p3, up, vec)


def _hgrn_call(p3, layer, vec):
    _, bsz, seq, _ = p3.shape
    t, n_sub = _mix_tiling(seq)
    chunk = HG_CHUNK_FIRST if layer == 0 else HG_CHUNK
    kern = functools.partial(_hgrn_kernel, t_blk=t // n_sub, n_sub=n_sub, chunk=min(chunk, t))
    return pl.pallas_call(
        kern,
        grid=(bsz, seq // t),
        in_specs=[_tile_spec(T_HG_Q, t), _tile_spec(T_HG_F, t), _tile_spec(T_HG_I, t),
                  _tile_spec(T_HG_Z, t), _layer_spec(layer, 8, TILE_W)],
        out_specs=pl.BlockSpec((None, t, BRANCH_W), lambda b, i: (b, i, 0)),
        out_shape=jax.ShapeDtypeStruct((bsz, seq, BRANCH_W), BF16),
        scratch_shapes=_gla_scratch(t // n_sub, TILE_W),
        compiler_params=_cparams(("parallel", "arbitrary")),
        name="hgrn2_mix",
    )(p3, p3, p3, p3, vec)


def _mlstm_kernel(qk_ref, v_ref, sm_ref, z_ref, conv_ref, vec_ref, o_ref,
                  prev_ref, ct_ref, n_ref, m_ref, *, t_blk, n_sub, chunk):
    @pl.when(pl.program_id(1) == 0)
    def _():
        prev_ref[...] = jnp.zeros_like(prev_ref)
        ct_ref[...] = jnp.zeros_like(ct_ref)
        n_ref[...] = jnp.zeros_like(n_ref)
        m_ref[...] = jnp.full_like(m_ref, STAB_INIT)

    for s in range(n_sub):
        _mlstm_tile(*[_sub(ref, s, t_blk) for ref in (qk_ref, v_ref, sm_ref, z_ref)], conv_ref, vec_ref,
                    _sub(o_ref, s, t_blk), prev_ref, ct_ref, n_ref, m_ref, t_blk=t_blk, chunk=chunk)


def _mlstm_tile(qk_ref, v_ref, sm_ref, z_ref, conv_ref, vec_ref, o_ref,
                prev_ref, ct_ref, n_ref, m_ref, *, t_blk, chunk):
    T, C = t_blk, chunk
    n_chunk = T // C
    heads = range(ML_H)

    x = qk_ref[...]
    xx = jnp.concatenate([prev_ref[...], x], axis=0)
    prev_ref[...] = x[T - 8:T, :]
    conv = x * conv_ref[ML_CONV - 1:ML_CONV, :]
    for d in range(1, ML_CONV):
        conv = conv + xx[8 - d:8 - d + T, :] * conv_ref[ML_CONV - 1 - d:ML_CONV - d, :]
    qk = _silu(conv)
    hq = ML_H * ML_DQK
    qb = qk[:, 0:hq].astype(BF16)
    kf = qk[:, hq:2 * hq] * (ML_DQK ** -0.5)
    kb = kf.astype(BF16)
    v = v_ref[...]
    vb = v.astype(BF16)
    v_tt = v.T.astype(BF16)

    gates = sm_ref[:, SM_GCODE:SM_GCODE + LANES]
    i_col = gates + vec_ref[0:1, 0:LANES]
    lf_col = _log_sigmoid(gates + vec_ref[1:2, 0:LANES])
    ri_t, ci_t, same_t = _chunk_masks(T, C)
    tri = jnp.where(same_t & (ci_t <= ri_t), 1.0, 0.0).astype(BF16)
    cf_col = _mask_dot_left(tri, lf_col)
    cf_row = cf_col.T
    i_row = i_col.T
    li, lf = SM_MI - SM_GCODE, SM_MF - SM_GCODE
    causal = _iota2((C, C), 1) <= _iota2((C, C), 0)
    lane_head = _iota2((1, LANES), 1) >> 6

    ct = [ct_ref[h * LANES:(h + 1) * LANES, :] for h in heads]
    n_row = [n_ref[h:h + 1, :] for h in heads]
    m_prev = [m_ref[h:h + 1, 0:1] for h in heads]
    outs = [[] for _ in heads]
    for c in range(n_chunk):
        tok = slice(c * C, (c + 1) * C)
        for h in heads:
            win = slice((h // 2) * LANES, (h // 2 + 1) * LANES)
            hm = lane_head == (h % 2)
            cf_i = cf_col[tok, lf + h:lf + h + 1]
            i_i = i_col[tok, li + h:li + h + 1]
            cf_j = cf_row[lf + h:lf + h + 1, tok]
            i_j = i_row[li + h:li + h + 1, tok]
            log_d = jnp.where(causal, cf_i - cf_j + i_j, -jnp.inf)
            log_inter = cf_i + m_prev[h]
            m_t = jnp.maximum(log_inter, jnp.max(log_d, axis=-1, keepdims=True))
            qh = jnp.where(hm, qb[tok, win], 0.0)
            s = _dot_nt(qh, kb[tok, win]) * jnp.exp(log_d - m_t)
            w_inter = jnp.exp(log_inter - m_t)
            num = (_dot(s.astype(BF16), vb[tok, h * LANES:(h + 1) * LANES])
                   + w_inter * _dot_nt(qh, ct[h].astype(BF16)))
            den = jnp.sum(s, axis=-1, keepdims=True) + w_inter * jnp.sum(
                jnp.where(hm, qk[tok, win], 0.0) * n_row[h], axis=-1, keepdims=True)
            outs[h].append(num / jnp.maximum(jnp.abs(den), jnp.exp(-m_t)))
            m_new = m_t[C - 1:C, :]
            cf_last = cf_i[C - 1:C, :]
            w_carry = jnp.exp(cf_last + m_prev[h] - m_new)
            kw = jnp.where(hm, kf[tok, win], 0.0) * jnp.exp(cf_last - cf_i + i_i - m_new)
            ct[h] = w_carry * ct[h] + _dot(v_tt[h * LANES:(h + 1) * LANES, tok], kw.astype(BF16))
            n_row[h] = w_carry * n_row[h] + jnp.sum(kw, axis=0, keepdims=True)
            m_prev[h] = m_new

    ln = []
    for h in heads:
        ct_ref[h * LANES:(h + 1) * LANES, :] = ct[h]
        n_ref[h:h + 1, :] = n_row[h]
        m_ref[h:h + 1, :] = jnp.broadcast_to(m_prev[h], (1, LANES))
        oh = jnp.concatenate(outs[h], axis=0)
        d = oh - jnp.mean(oh, axis=-1, keepdims=True)
        ln.append(d * lax.rsqrt(jnp.mean(d * d, axis=-1, keepdims=True) + NORM_EPS))
    o_ref[...] = (jnp.concatenate(ln, axis=1) * vec_ref[2:3, :] * _silu(z_ref[...])).astype(o_ref.dtype)


def _mlstm_call(p3, layer, conv_w, vec):
    _, bsz, seq, _ = p3.shape
    t, n_sub = _mix_tiling(seq)
    kern = functools.partial(_mlstm_kernel, t_blk=t // n_sub, n_sub=n_sub, chunk=min(ML_CHUNK, t))
    return pl.pallas_call(
        kern,
        grid=(bsz, seq // t),
        in_specs=[_tile_spec(T_ML_QK, t), _tile_spec(T_ML_V, t), _tile_spec(T_SMALL, t),
                  _tile_spec(T_ML_Z, t), _layer_spec(layer, 8, TILE_W), _layer_spec(layer, 8, TILE_W)],
        out_specs=pl.BlockSpec((None, t, BRANCH_W), lambda b, i: (b, i, 0)),
        out_shape=jax.ShapeDtypeStruct((bsz, seq, BRANCH_W), BF16),
        scratch_shapes=[pltpu.VMEM((8, TILE_W), F32), pltpu.VMEM((ML_H * LANES, LANES), F32),
                        pltpu.VMEM((8, LANES), F32), pltpu.VMEM((8, LANES), F32)],
        compiler_params=_cparams(("parallel", "arbitrary")),
        name="mlstm_mix",
    )(p3, p3, p3, p3, conv_w, vec)


def _merge_kernel(x_ref, ya_ref, yb_ref, yc_ref, yd_ref, *rest, final):
    gate_refs = rest[0:N_GATE_TILES]
    wb_ref, wo_ref, res_gate_ref, ng_ref, nsc_ref, nsh_ref = rest[N_GATE_TILES:N_GATE_TILES + 6]
    out_refs = rest[N_GATE_TILES + 6:]
    per_branch = N_GATE_TILES // N_BRANCH
    merged = None
    for m, y_ref in enumerate((ya_ref, yb_ref, yc_ref, yd_ref)):
        br = _dot(y_ref[...], wb_ref[m])
        gates = jnp.concatenate([gate_refs[per_branch * m + j][...] for j in range(per_branch)], axis=1)
        term = gates * br
        merged = term if merged is None else merged + term
    x_new = x_ref[...] + res_gate_ref[...] * _dot(merged.astype(BF16), wo_ref[...])
    ms = jnp.mean(x_new * x_new, axis=-1, keepdims=True)
    normed = x_new * lax.rsqrt(ms + NORM_EPS) * ng_ref[...]
    if final:
        out_refs[0][...] = normed
    else:
        out_refs[0][...] = x_new
        out_refs[1][...] = (normed * (1.0 + nsc_ref[...]) + nsh_ref[...]).astype(BF16)


def _merge_call(x, ys, gates, layer, w_branch, w_out, res_gate, next_g, next_scale, next_shift, final):
    bsz, seq, d = x.shape
    tm = min(seq, 512)
    row = pl.BlockSpec((None, tm, d), lambda b, i: (b, i, 0))
    y_spec = pl.BlockSpec((None, tm, BRANCH_W), lambda b, i: (b, i, 0))
    vec = pl.BlockSpec((None, 1, d), lambda b, i: (b, 0, 0))
    gate_specs = [_tile_spec(j, tm) for j in range(N_GATE_TILES)]
    out_shape = [jax.ShapeDtypeStruct((bsz, seq, d), F32)]
    out_specs = [row]
    if not final:
        out_shape.append(jax.ShapeDtypeStruct((bsz, seq, d), BF16))
        out_specs.append(row)
    return pl.pallas_call(
        functools.partial(_merge_kernel, final=final),
        grid=(bsz, seq // tm),
        in_specs=[row, y_spec, y_spec, y_spec, y_spec] + gate_specs
        + [pl.BlockSpec((None, N_BRANCH, BRANCH_W, d), lambda b, i: (layer, 0, 0, 0)),
           pl.BlockSpec((None, d, d), lambda b, i: (layer, 0, 0)),
           vec, pl.BlockSpec((1, d), lambda b, i: (0, 0)), vec, vec],
        out_specs=out_specs,
        out_shape=out_shape,
        compiler_params=_cparams(("parallel", "parallel")),
        name="merge_out",
    )(x, *ys, *([gates] * N_GATE_TILES), w_branch, w_out, res_gate.reshape(bsz, 1, d),
      next_g.reshape(1, d), next_scale.reshape(bsz, 1, d), next_shift.reshape(bsz, 1, d))


def _pad_last(a, width):
    return jnp.pad(a, [(0, 0)] * (a.ndim - 1) + [(0, width - a.shape[-1])])


def _rows(rows, width, n_rows=8):
    out = [_pad_last(r.astype(F32), width)[:, None, :] for r in rows]
    out.append(jnp.zeros((rows[0].shape[0], n_rows - len(rows), width), F32))
    return jnp.concatenate(out, axis=1)


def _relayout_w_in(w):
    o = 0

    def take(n):
        nonlocal o
        s = w[:, :, o:o + n]
        o += n
        return s

    rw_r, rw_k, rw_v, rw_wc, rw_ac, rw_z = take(512), take(512), take(512), take(64), take(64), take(512)
    gq, gk, gv, gcode, gz = take(256), take(256), take(512), take(16), take(512)
    mqk, mv, mi, mf, mz = take(512), take(512), take(4), take(4), take(512)
    hq, hf, hi, hz = take(512), take(512), take(512), take(512)
    merge = take(N_BRANCH * D_MODEL)
    small = _pad_last(jnp.concatenate([rw_wc, rw_ac, gcode, mi, mf], axis=2), TILE_W)
    tiles = [rw_r, rw_k, rw_v, rw_z, small, jnp.concatenate([gq, gk], axis=2), gv, gz,
             mqk, mv, mz, hq, hf, hi, hz]
    return jnp.concatenate(tiles, axis=2).astype(BF16), merge.astype(BF16)


def kernel(x, c, norm_g, ada_w, ada_b, w_in, rw_mu, rw_w0, rw_w_up, rw_a0, rw_a_up, rw_k_k, rw_k_a,
           rw_r_k, rw_ln_g, rw_ln_b, gla_gk_up, gla_gk_b, gla_norm_g, ml_conv_w, ml_i_b, ml_f_b,
           ml_norm_g, hg_lb_logits, hg_norm_g, w_branch, w_out, final_g):
    depth = w_in.shape[0]
    d = x.shape[-1]
    lb_p = jax.nn.softmax(hg_lb_logits.astype(F32), axis=0)
    lower_bounds = jnp.cumsum(lb_p, axis=0) - lb_p[0]

    w_mix, w_gate = _relayout_w_in(w_in)
    rw_mu_rows = _rows([rw_mu[:, 0:512], rw_mu[:, 512:1024], rw_mu[:, 1024:1536], rw_mu[:, 1536:1664]], TILE_W)
    rw_vec = _rows([rw_w0, rw_a0, rw_k_k, rw_k_a, rw_r_k, rw_ln_g, rw_ln_b], TILE_W)
    zeros = jnp.zeros((depth, RW_RANK, BRANCH_W), F32)
    rw_up = jnp.concatenate([jnp.concatenate([rw_w_up, zeros], axis=2),
                             jnp.concatenate([zeros, rw_a_up], axis=2)], axis=1).astype(BF16)
    gla_up = jnp.pad(gla_gk_up, ((0, 0), (0, LANES - GLA_RANK), (0, 0))).astype(BF16)
    gla_vec = _rows([gla_gk_b, gla_norm_g], TILE_W)
    ml_conv = jnp.pad(ml_conv_w.astype(F32), ((0, 0), (0, 8 - ML_CONV), (0, 0)))
    ml_vec = _rows([jnp.pad(ml_i_b, ((0, 0), (SM_MI - SM_GCODE, 0))),
                    jnp.pad(ml_f_b, ((0, 0), (SM_MF - SM_GCODE, 0))), ml_norm_g], TILE_W)
    hg_vec = _rows([lower_bounds, hg_norm_g], TILE_W)
    wb = w_branch.astype(BF16)
    wo = w_out.astype(BF16)

    mod = _ada_call(c, ada_w, ada_b)
    u = _norm_call(x, norm_g[0], mod[0, :, d:2 * d], mod[0, :, 0:d])
    for l in range(depth):
        p3 = _proj_call(u, w_mix, l, group=3, gate=False, name="in_proj")
        gates = _proj_call(u, w_gate, l, group=4, gate=True, name="gate_proj")
        y_a = _rwkv_call(p3, l, rw_mu_rows, rw_vec, rw_up)
        y_b = _gla_call(p3, l, gla_up, gla_vec)
        y_c = _mlstm_call(p3, l, ml_conv, ml_vec)
        y_d = _hgrn_call(p3, l, hg_vec)
        final = l == depth - 1
        if final:
            next_g, next_scale, next_shift = final_g, mod[l, :, d:2 * d], mod[l, :, 0:d]
        else:
            next_g, next_scale, next_shift = norm_g[l + 1], mod[l + 1, :, d:2 * d], mod[l + 1, :, 0:d]
        res = _merge_call(x, (y_a, y_b, y_c, y_d), gates, l, wb, wo, mod[l, :, 2 * d:3 * d],
                          next_g, next_scale, next_shift, final)
        if final:
            x = res[0]
        else:
            x, u = res
    return x
```

```python
import functools

import jax
import jax.numpy as jnp
import numpy as np
from jax import lax
from jax.experimental import pallas as pl
from jax.experimental.pallas import tpu as pltpu

F32 = jnp.float32
BF16 = jnp.bfloat16

D_MODEL = 1024
N_BRANCH = 4
BRANCH_W = 512
RW_N = 64
RW_H = 8
RW_RANK = 64
RW_GN_EPS = 64e-5
GLA_H = 4
GLA_DK = 64
GLA_RANK = 16
GLA_LOGIT_NORM = 16.0
ML_H = 4
ML_DQK = 64
ML_CONV = 4
STAB_INIT = -1e30
HG_H = 4
NORM_EPS = 1e-6

LANES = 128
TILE_W = 512
MIX_T = 256
RW_SUB = 2
FUSED_SUB = 4
SMALL_W = 128
RW_CHUNK = 64
RW_PAIR_T = 128
GLA_CHUNK = 128
HG_CHUNK_FIRST = 16
HG_CHUNK = 32
ML_CHUNK = 256
SAFE_EXP = 80.0
VMEM_LIMIT = 56 * 1024 * 1024

T_RW_R, T_RW_K, T_RW_V, T_RW_Z, T_RW_CODE = 0, 1, 2, 3, 4
N_RW_TILES = 5
N_GATE_TILES = N_BRANCH * D_MODEL // TILE_W
ML_I_LANE = 16
ML_F_LANE = 20


def _cparams(sem):
    return pltpu.CompilerParams(dimension_semantics=sem, vmem_limit_bytes=VMEM_LIMIT)


def _sigmoid(x):
    return 1.0 / (1.0 + jnp.exp(-x))


def _silu(x):
    return x * _sigmoid(x)


def _softplus(x):
    return jnp.maximum(x, 0.0) + jnp.log(1.0 + jnp.exp(-jnp.abs(x)))


def _log_sigmoid(x):
    return -_softplus(-x)


def _split2(x):
    hi = x.astype(BF16)
    lo = (x - hi.astype(F32)).astype(BF16)
    return hi, lo


def _dot(a, b):
    return jnp.dot(a, b, preferred_element_type=F32)


def _dot_nt(a, b):
    return lax.dot_general(a, b, (((1,), (1,)), ((), ())), preferred_element_type=F32)


def _mask_dot_left(m01, x):
    hi, lo = _split2(x)
    return _dot(m01, hi) + _dot(m01, lo)


def _iota2(shape, dim):
    return lax.broadcasted_iota(jnp.int32, shape, dim)


def _chunk_masks(t, c):
    sh = int(np.log2(c))
    ri = _iota2((t, t), 0)
    ci = _iota2((t, t), 1)
    same = (ri >> sh) == (ci >> sh)
    return ri, ci, same


def _seg_ones(width, seg):
    sh = int(np.log2(seg))
    ri = _iota2((width, width), 0)
    ci = _iota2((width, width), 1)
    return jnp.where((ri >> sh) == (ci >> sh), 1.0, 0.0).astype(BF16)


def _rms_mod(x, g, scale, shift):
    ms = jnp.mean(x * x, axis=-1, keepdims=True)
    return x * lax.rsqrt(ms + NORM_EPS) * g * (1.0 + scale) + shift


def _ada_kernel(c_ref, w_ref, b_ref, o_ref):
    cond = _silu(c_ref[...]).astype(BF16)
    o_ref[...] = _dot(cond, w_ref[...].astype(BF16)) + b_ref[...]


def _ada_call(c, ada_w, ada_b):
    depth, d, n3 = ada_w.shape
    bsz = c.shape[0]
    tn = 1024
    return pl.pallas_call(
        _ada_kernel,
        grid=(depth, n3 // tn),
        in_specs=[
            pl.BlockSpec((bsz, d), lambda l, j: (0, 0)),
            pl.BlockSpec((None, d, tn), lambda l, j: (l, 0, j)),
            pl.BlockSpec((None, 1, tn), lambda l, j: (l, 0, j)),
        ],
        out_specs=pl.BlockSpec((None, bsz, tn), lambda l, j: (l, 0, j)),
        out_shape=jax.ShapeDtypeStruct((depth, bsz, n3), F32),
        compiler_params=_cparams(("parallel", "parallel")),
        name="ada_mod",
    )(c, ada_w, ada_b.reshape(depth, 1, n3))


def _norm_kernel(x_ref, g_ref, sc_ref, sh_ref, u_ref):
    u_ref[...] = _rms_mod(x_ref[...], g_ref[...], sc_ref[...], sh_ref[...]).astype(u_ref.dtype)


def _norm_call(x, g, scale, shift):
    bsz, seq, d = x.shape
    tm = min(seq, 1024)
    return pl.pallas_call(
        _norm_kernel,
        grid=(bsz, seq // tm),
        in_specs=[pl.BlockSpec((None, tm, d), lambda b, i: (b, i, 0)),
                  pl.BlockSpec((1, d), lambda b, i: (0, 0)),
                  pl.BlockSpec((None, 1, d), lambda b, i: (b, 0, 0)),
                  pl.BlockSpec((None, 1, d), lambda b, i: (b, 0, 0))],
        out_specs=pl.BlockSpec((None, tm, d), lambda b, i: (b, i, 0)),
        out_shape=jax.ShapeDtypeStruct((bsz, seq, d), BF16),
        compiler_params=_cparams(("parallel", "parallel")),
        name="norm_mod",
    )(x, g.reshape(1, d), scale.reshape(bsz, 1, d), shift.reshape(bsz, 1, d))


def _proj_kernel(u_ref, w_ref, o_ref, *, group, gate):
    u = u_ref[...]
    for t in range(group):
        r = _dot(u, w_ref[:, t * TILE_W:(t + 1) * TILE_W])
        if gate:
            r = _sigmoid(r)
        o_ref[t] = r.astype(o_ref.dtype)


def _proj_call(u, w_tiles, layer, *, group, rows, gate, name):
    bsz, seq, d = u.shape
    n_tiles = w_tiles.shape[2] // TILE_W
    tm = min(seq, rows)
    return pl.pallas_call(
        functools.partial(_proj_kernel, group=group, gate=gate),
        grid=(bsz, seq // tm, n_tiles // group),
        in_specs=[pl.BlockSpec((None, tm, d), lambda b, i, j: (b, i, 0)),
                  pl.BlockSpec((None, d, group * TILE_W), lambda b, i, j: (layer, 0, j))],
        out_specs=pl.BlockSpec((group, None, tm, TILE_W), lambda b, i, j: (j, b, i, 0)),
        out_shape=jax.ShapeDtypeStruct((n_tiles, bsz, seq, TILE_W), BF16 if gate else F32),
        compiler_params=_cparams(("parallel", "parallel", "arbitrary")),
        name=name,
    )(u, w_tiles)


def _mix_tiling(seq, want_sub):
    sub_t = min(seq, MIX_T)
    n_sub = want_sub
    while seq % (sub_t * n_sub):
        n_sub //= 2
    return sub_t * n_sub, n_sub


def _sub(ref, s, t):
    return ref.at[pl.ds(s * t, t), :]


def _u_spec(t, d):
    return pl.BlockSpec((None, t, d), lambda b, i: (b, i, 0))


def _run_fused_tiles(tile_fn, widths, u_ref, w_ref, n_sub, t_blk):
    def project(s):
        u = u_ref[pl.ds(s * t_blk, t_blk), :]
        outs, lo = [], 0
        for wd in widths:
            outs.append(_dot(u, w_ref[:, lo:lo + wd]))
            lo += wd
        return outs

    pending = project(0)
    for s in range(n_sub):
        current = pending
        if s + 1 < n_sub:
            pending = project(s + 1)
        tile_fn(s, *current)


def _tile_spec(tile_id, t):
    return pl.BlockSpec((None, None, t, TILE_W), lambda b, i: (tile_id, b, i, 0))


def _layer_spec(layer, rows, width):
    return pl.BlockSpec((None, rows, width), lambda b, i: (layer, 0, 0))


def _block_inverse_all(n0s, ri, ci, c):
    eye = jnp.where(ri == ci, 1.0, 0.0)
    lvl1 = (ri >> 1) == (ci >> 1)
    xs = [eye + jnp.where(lvl1, n0, 0.0) for n0 in n0s]
    for lv in range(2, int(np.log2(c)) + 1):
        m = ((ri >> lv) == (ci >> lv)) & ((ri >> (lv - 1)) != (ci >> (lv - 1)))
        xbs = [x.astype(BF16) for x in xs]
        xns = [_dot(xb, jnp.where(m, n0, 0.0).astype(BF16)).astype(BF16) for xb, n0 in zip(xbs, n0s)]
        xs = [x + _dot(xn, xb) for x, xn, xb in zip(xs, xns, xbs)]
    return xs


def _rwkv_kernel(r_ref, k_ref, v_ref, sm_ref, z_ref, mu_ref, vec_ref, up_ref, o_ref,
                 carry_ref, st_ref, *, t_blk, n_sub, chunk, pair_t):
    @pl.when(pl.program_id(1) == 0)
    def _():
        carry_ref[...] = jnp.zeros_like(carry_ref)
        st_ref[...] = jnp.zeros_like(st_ref)

    for s in range(n_sub):
        _rwkv_tile(*[_sub(ref, s, t_blk) for ref in (r_ref, k_ref, v_ref, sm_ref, z_ref)],
                   mu_ref, vec_ref, up_ref, _sub(o_ref, s, t_blk), carry_ref, st_ref,
                   t_blk=t_blk, chunk=chunk, pair_t=pair_t)


def _rwkv_tile(r_ref, k_ref, v_ref, sm_ref, z_ref, mu_ref, vec_ref, up_ref, o_ref,
               carry_ref, st_ref, *, t_blk, chunk, pair_t):
    T, C, TP = t_blk, chunk, pair_t
    n_chunk = T // C
    n_part = T // TP
    c_per_part = TP // C

    row0 = _iota2((T, 1), 0) == 0

    def shift_mix(x, slot, width, mu):
        prev = carry_ref[slot:slot + 1, 0:width]
        xp = jnp.where(row0, prev, pltpu.roll(x, 1, 0))
        carry_ref[slot:slot + 1, 0:width] = x[T - 1:T, :]
        return x + (xp - x) * mu

    r = shift_mix(r_ref[...], 0, TILE_W, mu_ref[0:1, :])
    k = shift_mix(k_ref[...], 1, TILE_W, mu_ref[1:2, :])
    v = shift_mix(v_ref[...], 2, TILE_W, mu_ref[2:3, :])
    codes = shift_mix(sm_ref[:, 0:LANES], 3, LANES, mu_ref[3:4, 0:LANES])

    w0, a0 = vec_ref[0:1, :], vec_ref[1:2, :]
    k_k, k_a, r_k = vec_ref[2:3, :], vec_ref[3:4, :], vec_ref[4:5, :]
    ln_g, ln_b = vec_ref[5:6, :], vec_ref[6:7, :]

    lane = _iota2((T, LANES), 1)
    codes_t = jnp.where(lane < RW_RANK, jnp.tanh(codes), codes).astype(BF16)
    proj = _dot(codes_t, up_ref[...])
    w = -_softplus(-(w0 + proj[:, 0:TILE_W])) - 0.5
    lw = -jnp.exp(w)
    a = _sigmoid(a0 + proj[:, TILE_W:2 * TILE_W])

    ones64 = _seg_ones(TILE_W, RW_N)
    kkr = k * k_k
    ss = _dot((kkr * kkr).astype(BF16), ones64)
    kk = kkr / jnp.maximum(jnp.sqrt(ss), 1e-12)
    k2 = k * (1.0 + (a - 1.0) * k_a)
    alpha = -kk
    beta = kk * a

    ri_t, ci_t, same_t = _chunk_masks(T, C)
    tri = jnp.where(same_t & (ci_t <= ri_t), 1.0, 0.0).astype(BF16)
    blk = jnp.where(same_t, 1.0, 0.0).astype(BF16)
    cw = _mask_dot_left(tri, lw)
    cwl = _mask_dot_left(blk, lw)
    e_in = jnp.exp(cw)
    e_out = jnp.exp(-cw)
    e_end = jnp.exp(cwl - cw)
    a_t = alpha * jnp.exp(cw - lw)
    r_t = r * e_in
    b_t = beta * e_out
    k_t = k2 * e_out
    b_h = (beta * e_end).astype(BF16)
    k_h = (k2 * e_end).astype(BF16)
    p_tot = jnp.exp(cwl)

    a_tt = a_t.T.astype(BF16)
    r_tt = r_t.T.astype(BF16)
    v_tt = v.T.astype(BF16)

    ri, ci, same = _chunk_masks(TP, C)
    up_strict = same & (ci > ri)
    up_incl = same & (ci >= ri)
    lane_head = _iota2((1, LANES), 1) >> 6
    heads = range(RW_H)
    units = [(p, h) for p in range(n_part) for h in heads]

    def hrows(h):
        return slice(h * RW_N, (h + 1) * RW_N)

    m_ba, m_br, m_k = {}, {}, {}
    for u in units:
        p, h = u
        tok = slice(p * TP, (p + 1) * TP)
        win = slice((h // 2) * LANES, (h // 2 + 1) * LANES)
        hmask = lane_head == (h % 2)
        lhs = jnp.concatenate([jnp.where(hmask, b_t[tok, win], 0.0),
                               jnp.where(hmask, k_t[tok, win], 0.0)], axis=0).astype(BF16)
        rhs = jnp.concatenate([a_tt[win, tok], r_tt[win, tok]], axis=1)
        m = _dot(lhs, rhs)
        m_ba[u] = jnp.where(up_strict, m[0:TP, 0:TP], 0.0)
        m_br[u] = jnp.where(up_incl, m[0:TP, TP:2 * TP], 0.0).astype(BF16)
        m_k[u] = jnp.concatenate([jnp.where(up_strict, m[TP:2 * TP, 0:TP], 0.0),
                                  jnp.where(up_incl, m[TP:2 * TP, TP:2 * TP], 0.0)], axis=1).astype(BF16)
    inv = _block_inverse_all([m_ba[u] for u in units], ri, ci, C)
    tts = {u: x.astype(BF16) for u, x in zip(units, inv)}
    g2 = {(p, h): _dot(v_tt[hrows(h), p * TP:(p + 1) * TP], m_k[(p, h)]) for p, h in units}
    wu = {(p, h): _dot(jnp.concatenate([a_tt[hrows(h), p * TP:(p + 1) * TP],
                                        g2[(p, h)][:, 0:TP].astype(BF16)], axis=0), tts[(p, h)])
          for p, h in units}
    w_t = {u: wu[u][0:RW_N, :].astype(BF16) for u in units}
    kv = [[_dot(v_tt[hrows(h), c * C:(c + 1) * C], k_h[c * C:(c + 1) * C, hrows(h)])
           for c in range(n_chunk)] for h in heads]

    st = [st_ref[hrows(h), :] for h in heads]
    yts = [[] for _ in heads]
    for c in range(n_chunk):
        c0 = c * C
        p = c // c_per_part
        cl = (c % c_per_part) * C
        d1 = [_dot(st[h].astype(BF16),
                   jnp.concatenate([w_t[(p, h)][:, cl:cl + C], r_tt[hrows(h), c0:c0 + C]], axis=1))
              for h in heads]
        ut = [(d1[h][:, 0:C] + wu[(p, h)][RW_N:2 * RW_N, cl:cl + C]).astype(BF16) for h in heads]
        d2 = [_dot(ut[h], jnp.concatenate([m_br[(p, h)][cl:cl + C, cl:cl + C],
                                            b_h[c0:c0 + C, hrows(h)]], axis=1))
              for h in heads]
        for h in heads:
            yts[h].append(d1[h][:, C:2 * C] + g2[(p, h)][:, TP + cl:TP + cl + C] + d2[h][:, 0:C])
            st[h] = st[h] * p_tot[c0:c0 + 1, hrows(h)] + d2[h][:, C:C + RW_N] + kv[h][c]
    yt_heads = []
    for h in heads:
        st_ref[hrows(h), :] = st[h]
        yt = jnp.concatenate(yts[h], axis=1)
        d = yt - jnp.mean(yt, axis=0, keepdims=True)
        yt_heads.append(d * lax.rsqrt(jnp.mean(d * d, axis=0, keepdims=True) + RW_GN_EPS))
    y = jnp.concatenate(yt_heads, axis=0).T * ln_g + ln_b
    bonus = _dot((r * k2 * r_k).astype(BF16), ones64) * v
    o_ref[...] = ((y + bonus) * _silu(z_ref[...])).astype(o_ref.dtype)


def _rwkv_call(p3, layer, mu, vec, up):
    _, bsz, seq, _ = p3.shape
    t, n_sub = _mix_tiling(seq, RW_SUB)
    kern = functools.partial(_rwkv_kernel, t_blk=t // n_sub, n_sub=n_sub, chunk=min(RW_CHUNK, t),
                             pair_t=min(RW_PAIR_T, t))
    return pl.pallas_call(
        kern,
        grid=(bsz, seq // t),
        in_specs=[_tile_spec(T_RW_R, t), _tile_spec(T_RW_K, t), _tile_spec(T_RW_V, t),
                  _tile_spec(T_RW_CODE, t), _tile_spec(T_RW_Z, t),
                  _layer_spec(layer, 8, TILE_W), _layer_spec(layer, 8, TILE_W),
                  _layer_spec(layer, LANES, 2 * TILE_W)],
        out_specs=pl.BlockSpec((None, t, BRANCH_W), lambda b, i: (b, i, 0)),
        out_shape=jax.ShapeDtypeStruct((bsz, seq, BRANCH_W), BF16),
        scratch_shapes=[pltpu.VMEM((8, TILE_W), F32), pltpu.VMEM((RW_H * RW_N, RW_N), F32)],
        compiler_params=_cparams(("parallel", "arbitrary")),
        name="rwkv7_mix",
    )(p3, p3, p3, p3, p3, mu, vec, up)


def _gla_recurrence(q, k, v, logg, st_ref, qs_ref, ks_ref, vs_ref, gs_ref, os_ref, *,
                    n_head, dk, t_blk, chunk):
    T, C = t_blk, chunk
    n_chunk = T // C
    dv = LANES

    def window(h):
        if dk == LANES:
            return h * LANES, None
        return (h // 2) * LANES, (_iota2((1, LANES), 1) >> 6) == (h % 2)

    ri, ci, same = _chunk_masks(T, C)
    tri = jnp.where(same & (ci <= ri), 1.0, 0.0).astype(BF16)
    blk = jnp.where(same, 1.0, 0.0).astype(BF16)
    b = _mask_dot_left(tri, logg)
    btot = _mask_dot_left(blk, logg)
    worst = jnp.max(jnp.max(-btot, axis=1, keepdims=True), axis=0, keepdims=True)
    heads = range(n_head)
    chunks = range(n_chunk)
    st_in = [st_ref[h * dv:(h + 1) * dv, :] for h in heads]

    q_t = (q * jnp.exp(b)).astype(BF16)
    k_t = k * jnp.exp(-b)
    k_h = k * jnp.exp(btot - b)
    d_tot = jnp.exp(btot)
    v_tt = v.T.astype(BF16)
    vb = v.astype(BF16)
    causal = same & (ci <= ri)
    qw, ktw, khw = [], [], []
    for h in heads:
        w0, hm = window(h)
        qw.append(q_t[:, w0:w0 + LANES])
        kt_h, kh_h = k_t[:, w0:w0 + LANES], k_h[:, w0:w0 + LANES]
        if hm is not None:
            kt_h, kh_h = jnp.where(hm, kt_h, 0.0), jnp.where(hm, kh_h, 0.0)
        ktw.append(kt_h.astype(BF16))
        khw.append(kh_h.astype(BF16))
    att = [jnp.where(causal, _dot_nt(qw[h], ktw[h]), 0.0).astype(BF16) for h in heads]
    ds = [[_dot(v_tt[h * dv:(h + 1) * dv, c * C:(c + 1) * C], khw[h][c * C:(c + 1) * C, :]) for c in chunks]
          for h in heads]
    sts = []
    for h in heads:
        w0, _ = window(h)
        st = st_in[h]
        row = []
        for c in chunks:
            row.append(st.astype(BF16))
            st = st * d_tot[c * C:c * C + 1, w0:w0 + LANES] + ds[h][c]
        st_ref[h * dv:(h + 1) * dv, :] = st
        sts.append(row)
    for h in heads:
        inter = [_dot_nt(qw[h][c * C:(c + 1) * C, :], sts[h][c]) for c in chunks]
        os_ref[:, h * dv:(h + 1) * dv] = (_dot(att[h], vb[:, h * dv:(h + 1) * dv])
                                          + jnp.concatenate(inter, axis=0))

    @pl.when(worst[0, 0] > SAFE_EXP)
    def _():
        qs_ref[...] = q
        ks_ref[...] = k
        gs_ref[...] = logg
        vs_ref[...] = v
        eye = _iota2((LANES, LANES), 0) == _iota2((LANES, LANES), 1)
        for h in range(n_head):
            w0, hm = window(h)

            def body(t8, st, w0=w0, hm=hm, h=h):
                rows = pl.ds(pl.multiple_of(t8 * 8, 8), 8)
                q8 = qs_ref[rows, w0:w0 + LANES]
                k8 = ks_ref[rows, w0:w0 + LANES]
                e8 = jnp.exp(gs_ref[rows, w0:w0 + LANES])
                v8 = vs_ref[rows, h * dv:(h + 1) * dv]
                if hm is not None:
                    k8 = jnp.where(hm, k8, 0.0)
                    q8 = jnp.where(hm, q8, 0.0)
                o_rows = []
                for j in range(8):
                    v_col = jnp.sum(jnp.where(eye, v8[j:j + 1, :], 0.0), axis=1, keepdims=True)
                    st = st * e8[j:j + 1, :] + v_col * k8[j:j + 1, :]
                    o_col = jnp.sum(st * q8[j:j + 1, :], axis=1, keepdims=True)
                    o_rows.append(jnp.sum(jnp.where(eye, o_col, 0.0), axis=0, keepdims=True))
                os_ref[rows, h * dv:(h + 1) * dv] = jnp.concatenate(o_rows, axis=0)
                return st

            st_ref[h * dv:(h + 1) * dv, :] = lax.fori_loop(0, T // 8, body, st_in[h])


def _head_rms_gate(o, g, z, n_head):
    outs = []
    for h in range(n_head):
        oh = o[:, h * LANES:(h + 1) * LANES]
        outs.append(oh * lax.rsqrt(jnp.mean(oh * oh, axis=-1, keepdims=True) + NORM_EPS))
    return jnp.concatenate(outs, axis=1) * g * _silu(z)


def _gla_kernel(u_ref, w_ref, up_ref, vec_ref, o_ref, st_ref, *scratch, t_blk, n_sub, chunk):
    @pl.when(pl.program_id(1) == 0)
    def _():
        st_ref[...] = jnp.zeros_like(st_ref)

    def tile(s, qk, v, z, code):
        _gla_tile(qk, v, code, z, up_ref, vec_ref, _sub(o_ref, s, t_blk), st_ref, *scratch,
                  t_blk=t_blk, chunk=chunk)

    _run_fused_tiles(tile, (TILE_W, TILE_W, TILE_W, SMALL_W), u_ref, w_ref, n_sub, t_blk)


def _gla_tile(qk, v, code, z, up_ref, vec_ref, o_ref,
              st_ref, qs_ref, ks_ref, vs_ref, gs_ref, os_ref, *, t_blk, chunk):
    hk = GLA_H * GLA_DK
    q = qk[:, 0:hk] * (GLA_DK ** -0.5)
    k = qk[:, hk:2 * hk]
    logits = _dot(code.astype(BF16), up_ref[...]) + vec_ref[0:1, 0:hk]
    logg = _log_sigmoid(logits) / GLA_LOGIT_NORM
    _gla_recurrence(q, k, v, logg, st_ref, qs_ref, ks_ref, vs_ref, gs_ref, os_ref,
                    n_head=GLA_H, dk=GLA_DK, t_blk=t_blk, chunk=chunk)
    o_ref[...] = _head_rms_gate(os_ref[...], vec_ref[1:2, :], z, GLA_H).astype(o_ref.dtype)


def _hgrn_kernel(u_ref, w_ref, vec_ref, o_ref, st_ref, *scratch, t_blk, n_sub, chunk):
    @pl.when(pl.program_id(1) == 0)
    def _():
        st_ref[...] = jnp.zeros_like(st_ref)

    def tile(s, q, f, i, z):
        _hgrn_tile(q, f, i, z, vec_ref, _sub(o_ref, s, t_blk), st_ref, *scratch, t_blk=t_blk, chunk=chunk)

    _run_fused_tiles(tile, (TILE_W,) * 4, u_ref, w_ref, n_sub, t_blk)


def _hgrn_tile(q, f, i, z, vec_ref, o_ref,
               st_ref, qs_ref, ks_ref, vs_ref, gs_ref, os_ref, *, t_blk, chunk):
    lb = vec_ref[0:1, :]
    g = lb + (1.0 - lb) * _sigmoid(f)
    _gla_recurrence(_silu(q), 1.0 - g, i, jnp.log(g), st_ref,
                    qs_ref, ks_ref, vs_ref, gs_ref, os_ref,
                    n_head=HG_H, dk=LANES, t_blk=t_blk, chunk=chunk)
    o_ref[...] = _head_rms_gate(os_ref[...], vec_ref[1:2, :], z, HG_H).astype(o_ref.dtype)


def _gla_scratch(t, key_w):
    return [pltpu.VMEM((4 * LANES, LANES), F32), pltpu.VMEM((t, key_w), F32), pltpu.VMEM((t, key_w), F32),
            pltpu.VMEM((t, BRANCH_W), F32), pltpu.VMEM((t, key_w), F32), pltpu.VMEM((t, BRANCH_W), F32)]


def _gla_call(u, w, layer, up, vec):
    bsz, seq, d = u.shape
    t, n_sub = _mix_tiling(seq, FUSED_SUB)
    kern = functools.partial(_gla_kernel, t_blk=t // n_sub, n_sub=n_sub, chunk=min(GLA_CHUNK, t))
    return pl.pallas_call(
        kern,
        grid=(bsz, seq // t),
        in_specs=[_u_spec(t, d), _layer_spec(layer, d, w.shape[2]),
                  _layer_spec(layer, LANES, GLA_H * GLA_DK), _layer_spec(layer, 8, TILE_W)],
        out_specs=pl.BlockSpec((None, t, BRANCH_W), lambda b, i: (b, i, 0)),
        out_shape=jax.ShapeDtypeStruct((bsz, seq, BRANCH_W), BF16),
        scratch_shapes=_gla_scratch(t // n_sub, GLA_H * GLA_DK),
        compiler_params=_cparams(("parallel", "arbitrary")),
        name="gla_mix",
    )(u, w, up, vec)


def _hgrn_call(u, w, layer, vec):
    bsz, seq, d = u.shape
    t, n_sub = _mix_tiling(seq, FUSED_SUB)
    chunk = HG_CHUNK_FIRST if layer == 0 else HG_CHUNK
    kern = functools.partial(_hgrn_kernel, t_blk=t // n_sub, n_sub=n_sub, chunk=min(chunk, t))
    return pl.pallas_call(
        kern,
        grid=(bsz, seq // t),
        in_specs=[_u_spec(t, d), _layer_spec(layer, d, w.shape[2]), _layer_spec(layer, 8, TILE_W)],
        out_specs=pl.BlockSpec((None, t, BRANCH_W), lambda b, i: (b, i, 0)),
        out_shape=jax.ShapeDtypeStruct((bsz, seq, BRANCH_W), BF16),
        scratch_shapes=_gla_scratch(t // n_sub, TILE_W),
        compiler_params=_cparams(("parallel", "arbitrary")),
        name="hgrn2_mix",
    )(u, w, vec)


def _mlstm_kernel(u_ref, w_ref, conv_ref, vec_ref, o_ref,
                  prev_ref, ct_ref, n_ref, m_ref, *, t_blk, n_sub, chunk):
    @pl.when(pl.program_id(1) == 0)
    def _():
        prev_ref[...] = jnp.zeros_like(prev_ref)
        ct_ref[...] = jnp.zeros_like(ct_ref)
        n_ref[...] = jnp.zeros_like(n_ref)
        m_ref[...] = jnp.full_like(m_ref, STAB_INIT)

    def tile(s, qk, v, z, gates):
        _mlstm_tile(qk, v, gates, z, conv_ref, vec_ref, _sub(o_ref, s, t_blk), prev_ref, ct_ref, n_ref, m_ref,
                    t_blk=t_blk, chunk=chunk)

    _run_fused_tiles(tile, (TILE_W, TILE_W, TILE_W, SMALL_W), u_ref, w_ref, n_sub, t_blk)


def _mlstm_tile(x, v, gates, z, conv_ref, vec_ref, o_ref,
                prev_ref, ct_ref, n_ref, m_ref, *, t_blk, chunk):
    T, C = t_blk, chunk
    n_chunk = T // C
    heads = range(ML_H)

    xx = jnp.concatenate([prev_ref[...], x], axis=0)
    prev_ref[...] = x[T - 8:T, :]
    conv = x * conv_ref[ML_CONV - 1:ML_CONV, :]
    for d in range(1, ML_CONV):
        conv = conv + xx[8 - d:8 - d + T, :] * conv_ref[ML_CONV - 1 - d:ML_CONV - d, :]
    qk = _silu(conv)
    hq = ML_H * ML_DQK
    qb = qk[:, 0:hq].astype(BF16)
    kf = qk[:, hq:2 * hq] * (ML_DQK ** -0.5)
    kb = kf.astype(BF16)
    vb = v.astype(BF16)
    v_tt = v.T.astype(BF16)

    i_col = gates + vec_ref[0:1, 0:LANES]
    lf_col = _log_sigmoid(gates + vec_ref[1:2, 0:LANES])
    ri_t, ci_t, same_t = _chunk_masks(T, C)
    tri = jnp.where(same_t & (ci_t <= ri_t), 1.0, 0.0).astype(BF16)
    cf_col = _mask_dot_left(tri, lf_col)
    cf_row = cf_col.T
    i_row = i_col.T
    li, lf = ML_I_LANE, ML_F_LANE
    causal = _iota2((C, C), 1) <= _iota2((C, C), 0)
    lane_head = _iota2((1, LANES), 1) >> 6

    ct = [ct_ref[h * LANES:(h + 1) * LANES, :] for h in heads]
    n_row = [n_ref[h:h + 1, :] for h in heads]
    m_prev = [m_ref[h:h + 1, 0:1] for h in heads]
    outs = [[] for _ in heads]
    for c in range(n_chunk):
        tok = slice(c * C, (c + 1) * C)
        for h in heads:
            win = slice((h // 2) * LANES, (h // 2 + 1) * LANES)
            hm = lane_head == (h % 2)
            cf_i = cf_col[tok, lf + h:lf + h + 1]
            i_i = i_col[tok, li + h:li + h + 1]
            cf_j = cf_row[lf + h:lf + h + 1, tok]
            i_j = i_row[li + h:li + h + 1, tok]
            log_d = jnp.where(causal, cf_i - cf_j + i_j, -jnp.inf)
            log_inter = cf_i + m_prev[h]
            m_t = jnp.maximum(log_inter, jnp.max(log_d, axis=-1, keepdims=True))
            qh = jnp.where(hm, qb[tok, win], 0.0)
            s = _dot_nt(qh, kb[tok, win]) * jnp.exp(log_d - m_t)
            w_inter = jnp.exp(log_inter - m_t)
            num = (_dot(s.astype(BF16), vb[tok, h * LANES:(h + 1) * LANES])
                   + w_inter * _dot_nt(qh, ct[h].astype(BF16)))
            den = jnp.sum(s, axis=-1, keepdims=True) + w_inter * jnp.sum(
                jnp.where(hm, qk[tok, win], 0.0) * n_row[h], axis=-1, keepdims=True)
            outs[h].append(num / jnp.maximum(jnp.abs(den), jnp.exp(-m_t)))
            m_new = m_t[C - 1:C, :]
            cf_last = cf_i[C - 1:C, :]
            w_carry = jnp.exp(cf_last + m_prev[h] - m_new)
            kw = jnp.where(hm, kf[tok, win], 0.0) * jnp.exp(cf_last - cf_i + i_i - m_new)
            ct[h] = w_carry * ct[h] + _dot(v_tt[h * LANES:(h + 1) * LANES, tok], kw.astype(BF16))
            n_row[h] = w_carry * n_row[h] + jnp.sum(kw, axis=0, keepdims=True)
            m_prev[h] = m_new

    ln = []
    for h in heads:
        ct_ref[h * LANES:(h + 1) * LANES, :] = ct[h]
        n_ref[h:h + 1, :] = n_row[h]
        m_ref[h:h + 1, :] = jnp.broadcast_to(m_prev[h], (1, LANES))
        oh = jnp.concatenate(outs[h], axis=0)
        d = oh - jnp.mean(oh, axis=-1, keepdims=True)
        ln.append(d * lax.rsqrt(jnp.mean(d * d, axis=-1, keepdims=True) + NORM_EPS))
    o_ref[...] = (jnp.concatenate(ln, axis=1) * vec_ref[2:3, :] * _silu(z)).astype(o_ref.dtype)


def _mlstm_call(u, w, layer, conv_w, vec):
    bsz, seq, d = u.shape
    t, n_sub = _mix_tiling(seq, FUSED_SUB)
    kern = functools.partial(_mlstm_kernel, t_blk=t // n_sub, n_sub=n_sub, chunk=min(ML_CHUNK, t))
    return pl.pallas_call(
        kern,
        grid=(bsz, seq // t),
        in_specs=[_u_spec(t, d), _layer_spec(layer, d, w.shape[2]),
                  _layer_spec(layer, 8, TILE_W), _layer_spec(layer, 8, TILE_W)],
        out_specs=pl.BlockSpec((None, t, BRANCH_W), lambda b, i: (b, i, 0)),
        out_shape=jax.ShapeDtypeStruct((bsz, seq, BRANCH_W), BF16),
        scratch_shapes=[pltpu.VMEM((8, TILE_W), F32), pltpu.VMEM((ML_H * LANES, LANES), F32),
                        pltpu.VMEM((8, LANES), F32), pltpu.VMEM((8, LANES), F32)],
        compiler_params=_cparams(("parallel", "arbitrary")),
        name="mlstm_mix",
    )(u, w, conv_w, vec)


def _merge_kernel(x_ref, ya_ref, yb_ref, yc_ref, yd_ref, *rest, final):
    gate_refs = rest[0:N_GATE_TILES]
    wb_ref, wo_ref, res_gate_ref, ng_ref, nsc_ref, nsh_ref = rest[N_GATE_TILES:N_GATE_TILES + 6]
    out_refs = rest[N_GATE_TILES + 6:]
    per_branch = N_GATE_TILES // N_BRANCH
    merged = None
    for m, y_ref in enumerate((ya_ref, yb_ref, yc_ref, yd_ref)):
        br = _dot(y_ref[...], wb_ref[m])
        gates = jnp.concatenate([gate_refs[per_branch * m + j][...] for j in range(per_branch)], axis=1)
        term = gates * br
        merged = term if merged is None else merged + term
    x_new = x_ref[...] + res_gate_ref[...] * _dot(merged.astype(BF16), wo_ref[...])
    ms = jnp.mean(x_new * x_new, axis=-1, keepdims=True)
    normed = x_new * lax.rsqrt(ms + NORM_EPS) * ng_ref[...]
    if final:
        out_refs[0][...] = normed
    else:
        out_refs[0][...] = x_new
        out_refs[1][...] = (normed * (1.0 + nsc_ref[...]) + nsh_ref[...]).astype(BF16)


def _merge_call(x, ys, gates, layer, w_branch, w_out, res_gate, next_g, next_scale, next_shift, final):
    bsz, seq, d = x.shape
    tm = min(seq, 512)
    row = pl.BlockSpec((None, tm, d), lambda b, i: (b, i, 0))
    y_spec = pl.BlockSpec((None, tm, BRANCH_W), lambda b, i: (b, i, 0))
    vec = pl.BlockSpec((None, 1, d), lambda b, i: (b, 0, 0))
    gate_specs = [_tile_spec(j, tm) for j in range(N_GATE_TILES)]
    out_shape = [jax.ShapeDtypeStruct((bsz, seq, d), F32)]
    out_specs = [row]
    if not final:
        out_shape.append(jax.ShapeDtypeStruct((bsz, seq, d), BF16))
        out_specs.append(row)
    return pl.pallas_call(
        functools.partial(_merge_kernel, final=final),
        grid=(bsz, seq // tm),
        in_specs=[row, y_spec, y_spec, y_spec, y_spec] + gate_specs
        + [pl.BlockSpec((None, N_BRANCH, BRANCH_W, d), lambda b, i: (layer, 0, 0, 0)),
           pl.BlockSpec((None, d, d), lambda b, i: (layer, 0, 0)),
           vec, pl.BlockSpec((1, d), lambda b, i: (0, 0)), vec, vec],
        out_specs=out_specs,
        out_shape=out_shape,
        compiler_params=_cparams(("parallel", "parallel")),
        name="merge_out",
    )(x, *ys, *([gates] * N_GATE_TILES), w_branch, w_out, res_gate.reshape(bsz, 1, d),
      next_g.reshape(1, d), next_scale.reshape(bsz, 1, d), next_shift.reshape(bsz, 1, d))


def _pad_last(a, width):
    return jnp.pad(a, [(0, 0)] * (a.ndim - 1) + [(0, width - a.shape[-1])])


def _rows(rows, width, n_rows=8):
    out = [_pad_last(r.astype(F32), width)[:, None, :] for r in rows]
    out.append(jnp.zeros((rows[0].shape[0], n_rows - len(rows), width), F32))
    return jnp.concatenate(out, axis=1)


def _relayout_w_in(w):
    o = 0

    def take(n):
        nonlocal o
        s = w[:, :, o:o + n]
        o += n
        return s

    rw_r, rw_k, rw_v, rw_wc, rw_ac, rw_z = take(512), take(512), take(512), take(64), take(64), take(512)
    gq, gk, gv, gcode, gz = take(256), take(256), take(512), take(16), take(512)
    mqk, mv, mi, mf, mz = take(512), take(512), take(4), take(4), take(512)
    hq, hf, hi, hz = take(512), take(512), take(512), take(512)
    merge = take(N_BRANCH * D_MODEL)
    lead = jnp.zeros(w.shape[:2] + (ML_I_LANE,), w.dtype)
    blocks = {
        "rw": [rw_r, rw_k, rw_v, rw_z, _pad_last(jnp.concatenate([rw_wc, rw_ac], axis=2), TILE_W)],
        "gla": [gq, gk, gv, gz, _pad_last(gcode, SMALL_W)],
        "ml": [mqk, mv, mz, _pad_last(jnp.concatenate([lead, mi, mf], axis=2), SMALL_W)],
        "hg": [hq, hf, hi, hz],
        "gate": [merge],
    }
    return {name: jnp.concatenate(cols, axis=2).astype(BF16) for name, cols in blocks.items()}


def kernel(x, c, norm_g, ada_w, ada_b, w_in, rw_mu, rw_w0, rw_w_up, rw_a0, rw_a_up, rw_k_k, rw_k_a,
           rw_r_k, rw_ln_g, rw_ln_b, gla_gk_up, gla_gk_b, gla_norm_g, ml_conv_w, ml_i_b, ml_f_b,
           ml_norm_g, hg_lb_logits, hg_norm_g, w_branch, w_out, final_g):
    depth = w_in.shape[0]
    d = x.shape[-1]
    lb_p = jax.nn.softmax(hg_lb_logits.astype(F32), axis=0)
    lower_bounds = jnp.cumsum(lb_p, axis=0) - lb_p[0]

    w_cols = _relayout_w_in(w_in)
    rw_mu_rows = _rows([rw_mu[:, 0:512], rw_mu[:, 512:1024], rw_mu[:, 1024:1536], rw_mu[:, 1536:1664]], TILE_W)
    rw_vec = _rows([rw_w0, rw_a0, rw_k_k, rw_k_a, rw_r_k, rw_ln_g, rw_ln_b], TILE_W)
    zeros = jnp.zeros((depth, RW_RANK, BRANCH_W), F32)
    rw_up = jnp.concatenate([jnp.concatenate([rw_w_up, zeros], axis=2),
                             jnp.concatenate([zeros, rw_a_up], axis=2)], axis=1).astype(BF16)
    gla_up = jnp.pad(gla_gk_up, ((0, 0), (0, LANES - GLA_RANK), (0, 0))).astype(BF16)
    gla_vec = _rows([gla_gk_b, gla_norm_g], TILE_W)
    ml_conv = jnp.pad(ml_conv_w.astype(F32), ((0, 0), (0, 8 - ML_CONV), (0, 0)))
    ml_vec = _rows([jnp.pad(ml_i_b, ((0, 0), (ML_I_LANE, 0))),
                    jnp.pad(ml_f_b, ((0, 0), (ML_F_LANE, 0))), ml_norm_g], TILE_W)
    hg_vec = _rows([lower_bounds, hg_norm_g], TILE_W)
    wb = w_branch.astype(BF16)
    wo = w_out.astype(BF16)

    mod = _ada_call(c, ada_w, ada_b)
    u = _norm_call(x, norm_g[0], mod[0, :, d:2 * d], mod[0, :, 0:d])
    for l in range(depth):
        p3 = _proj_call(u, w_cols["rw"], l, group=N_RW_TILES, rows=1024, gate=False, name="in_proj")
        gates = _proj_call(u, w_cols["gate"], l, group=4, rows=2048, gate=True, name="gate_proj")
        y_a = _rwkv_call(p3, l, rw_mu_rows, rw_vec, rw_up)
        y_b = _gla_call(u, w_cols["gla"], l, gla_up, gla_vec)
        y_c = _mlstm_call(u, w_cols["ml"], l, ml_conv, ml_vec)
        y_d = _hgrn_call(u, w_cols["hg"], l, hg_vec)
        final = l == depth - 1
        if final:
            next_g, next_scale, next_shift = final_g, mod[l, :, d:2 * d], mod[l, :, 0:d]
        else:
            next_g, next_scale, next_shift = norm_g[l + 1], mod[l + 1, :, d:2 * d], mod[l + 1, :, 0:d]
        res = _merge_call(x, (y_a, y_b, y_c, y_d), gates, l, wb, wo, mod[l, :, 2 * d:3 * d],
                          next_g, next_scale, next_shift, final)
        if final:
            x = res[0]
        else:
            x, u = res
    return x
```

```python
import functools

import jax
import jax.numpy as jnp
import numpy as np
from jax import lax
from jax.experimental import pallas as pl
from jax.experimental.pallas import tpu as pltpu

F32 = jnp.float32
BF16 = jnp.bfloat16

D_MODEL = 1024
N_BRANCH = 4
BRANCH_W = 512
RW_N = 64
RW_H = 8
RW_RANK = 64
RW_GN_EPS = 64e-5
GLA_H = 4
GLA_DK = 64
GLA_RANK = 16
GLA_LOGIT_NORM = 16.0
ML_H = 4
ML_DQK = 64
ML_CONV = 4
STAB_INIT = -1e30
HG_H = 4
NORM_EPS = 1e-6

LANES = 128
TILE_W = 512
MIX_T = 256
RW_SUB = 2
FUSED_SUB = 4
SMALL_W = 128
RW_CHUNK = 64
RW_PAIR_T = 128
GLA_CHUNK = 128
HG_CHUNK_FIRST = 16
HG_CHUNK = 32
ML_CHUNK = 256
SAFE_EXP = 80.0
VMEM_LIMIT = 56 * 1024 * 1024

N_GATE_TILES = N_BRANCH * D_MODEL // TILE_W
ML_I_LANE = 16
ML_F_LANE = 20


def _cparams(sem):
    return pltpu.CompilerParams(dimension_semantics=sem, vmem_limit_bytes=VMEM_LIMIT)


def _sigmoid(x):
    return 1.0 / (1.0 + jnp.exp(-x))


def _silu(x):
    return x * _sigmoid(x)


def _softplus(x):
    return jnp.maximum(x, 0.0) + jnp.log(1.0 + jnp.exp(-jnp.abs(x)))


def _log_sigmoid(x):
    return -_softplus(-x)


def _split2(x):
    hi = x.astype(BF16)
    lo = (x - hi.astype(F32)).astype(BF16)
    return hi, lo


def _dot(a, b):
    return jnp.dot(a, b, preferred_element_type=F32)


def _dot_nt(a, b):
    return lax.dot_general(a, b, (((1,), (1,)), ((), ())), preferred_element_type=F32)


def _mask_dot_left(m01, x):
    hi, lo = _split2(x)
    return _dot(m01, hi) + _dot(m01, lo)


def _iota2(shape, dim):
    return lax.broadcasted_iota(jnp.int32, shape, dim)


def _chunk_masks(t, c):
    sh = int(np.log2(c))
    ri = _iota2((t, t), 0)
    ci = _iota2((t, t), 1)
    same = (ri >> sh) == (ci >> sh)
    return ri, ci, same


def _seg_ones(width, seg):
    sh = int(np.log2(seg))
    ri = _iota2((width, width), 0)
    ci = _iota2((width, width), 1)
    return jnp.where((ri >> sh) == (ci >> sh), 1.0, 0.0).astype(BF16)


def _rms_mod(x, g, scale, shift):
    ms = jnp.mean(x * x, axis=-1, keepdims=True)
    return x * lax.rsqrt(ms + NORM_EPS) * g * (1.0 + scale) + shift


def _ada_kernel(c_ref, w_ref, b_ref, o_ref):
    cond = _silu(c_ref[...]).astype(BF16)
    o_ref[...] = _dot(cond, w_ref[...].astype(BF16)) + b_ref[...]


def _ada_call(c, ada_w, ada_b):
    depth, d, n3 = ada_w.shape
    bsz = c.shape[0]
    tn = 1024
    return pl.pallas_call(
        _ada_kernel,
        grid=(depth, n3 // tn),
        in_specs=[
            pl.BlockSpec((bsz, d), lambda l, j: (0, 0)),
            pl.BlockSpec((None, d, tn), lambda l, j: (l, 0, j)),
            pl.BlockSpec((None, 1, tn), lambda l, j: (l, 0, j)),
        ],
        out_specs=pl.BlockSpec((None, bsz, tn), lambda l, j: (l, 0, j)),
        out_shape=jax.ShapeDtypeStruct((depth, bsz, n3), F32),
        compiler_params=_cparams(("parallel", "parallel")),
        name="ada_mod",
    )(c, ada_w, ada_b.reshape(depth, 1, n3))


def _norm_kernel(x_ref, g_ref, sc_ref, sh_ref, u_ref):
    u_ref[...] = _rms_mod(x_ref[...], g_ref[...], sc_ref[...], sh_ref[...]).astype(u_ref.dtype)


def _norm_call(x, g, scale, shift):
    bsz, seq, d = x.shape
    tm = min(seq, 1024)
    return pl.pallas_call(
        _norm_kernel,
        grid=(bsz, seq // tm),
        in_specs=[pl.BlockSpec((None, tm, d), lambda b, i: (b, i, 0)),
                  pl.BlockSpec((1, d), lambda b, i: (0, 0)),
                  pl.BlockSpec((None, 1, d), lambda b, i: (b, 0, 0)),
                  pl.BlockSpec((None, 1, d), lambda b, i: (b, 0, 0))],
        out_specs=pl.BlockSpec((None, tm, d), lambda b, i: (b, i, 0)),
        out_shape=jax.ShapeDtypeStruct((bsz, seq, d), BF16),
        compiler_params=_cparams(("parallel", "parallel")),
        name="norm_mod",
    )(x, g.reshape(1, d), scale.reshape(bsz, 1, d), shift.reshape(bsz, 1, d))


def _proj_kernel(u_ref, w_ref, o_ref, *, group, gate):
    u = u_ref[...]
    for t in range(group):
        r = _dot(u, w_ref[:, t * TILE_W:(t + 1) * TILE_W])
        if gate:
            r = _sigmoid(r)
        o_ref[t] = r.astype(o_ref.dtype)


def _proj_call(u, w_tiles, layer, *, group, rows, gate, name):
    bsz, seq, d = u.shape
    n_tiles = w_tiles.shape[2] // TILE_W
    tm = min(seq, rows)
    return pl.pallas_call(
        functools.partial(_proj_kernel, group=group, gate=gate),
        grid=(bsz, seq // tm, n_tiles // group),
        in_specs=[pl.BlockSpec((None, tm, d), lambda b, i, j: (b, i, 0)),
                  pl.BlockSpec((None, d, group * TILE_W), lambda b, i, j: (layer, 0, j))],
        out_specs=pl.BlockSpec((group, None, tm, TILE_W), lambda b, i, j: (j, b, i, 0)),
        out_shape=jax.ShapeDtypeStruct((n_tiles, bsz, seq, TILE_W), BF16 if gate else F32),
        compiler_params=_cparams(("parallel", "parallel", "arbitrary")),
        name=name,
    )(u, w_tiles)


def _mix_tiling(seq, want_sub):
    sub_t = min(seq, MIX_T)
    n_sub = want_sub
    while seq % (sub_t * n_sub):
        n_sub //= 2
    return sub_t * n_sub, n_sub


def _sub(ref, s, t):
    return ref.at[pl.ds(s * t, t), :]


def _u_spec(t, d):
    return pl.BlockSpec((None, t, d), lambda b, i: (b, i, 0))


def _run_fused_tiles(tile_fn, widths, u_ref, w_ref, n_sub, t_blk):
    def project(s):
        u = u_ref[pl.ds(s * t_blk, t_blk), :]
        outs, lo = [], 0
        for wd in widths:
            outs.append(_dot(u, w_ref[:, lo:lo + wd]))
            lo += wd
        return outs

    pending = project(0)
    for s in range(n_sub):
        current = pending
        if s + 1 < n_sub:
            pending = project(s + 1)
        tile_fn(s, *current)


def _tile_spec(tile_id, t):
    return pl.BlockSpec((None, None, t, TILE_W), lambda b, i: (tile_id, b, i, 0))


def _layer_spec(layer, rows, width):
    return pl.BlockSpec((None, rows, width), lambda b, i: (layer, 0, 0))


def _block_inverse_all(n0s, ri, ci, c):
    eye = jnp.where(ri == ci, 1.0, 0.0)
    lvl1 = (ri >> 1) == (ci >> 1)
    xs = [eye + jnp.where(lvl1, n0, 0.0) for n0 in n0s]
    for lv in range(2, int(np.log2(c)) + 1):
        m = ((ri >> lv) == (ci >> lv)) & ((ri >> (lv - 1)) != (ci >> (lv - 1)))
        xbs = [x.astype(BF16) for x in xs]
        xns = [_dot(xb, jnp.where(m, n0, 0.0).astype(BF16)).astype(BF16) for xb, n0 in zip(xbs, n0s)]
        xs = [x + _dot(xn, xb) for x, xn, xb in zip(xs, xns, xbs)]
    return xs


def _rwkv_kernel(u_ref, w_ref, mu_ref, vec_ref, up_ref, o_ref,
                 carry_ref, st_ref, *, t_blk, n_sub, chunk, pair_t):
    @pl.when(pl.program_id(1) == 0)
    def _():
        carry_ref[...] = jnp.zeros_like(carry_ref)
        st_ref[...] = jnp.zeros_like(st_ref)

    def tile(s, r, k, v, z, codes):
        _rwkv_tile(r, k, v, codes, z, mu_ref, vec_ref, up_ref, _sub(o_ref, s, t_blk), carry_ref, st_ref,
                   t_blk=t_blk, chunk=chunk, pair_t=pair_t)

    _run_fused_tiles(tile, (TILE_W, TILE_W, TILE_W, TILE_W, SMALL_W), u_ref, w_ref, n_sub, t_blk)


def _rwkv_tile(r_ref, k_ref, v_ref, sm_ref, z_ref, mu_ref, vec_ref, up_ref, o_ref,
               carry_ref, st_ref, *, t_blk, chunk, pair_t):
    T, C, TP = t_blk, chunk, pair_t
    n_chunk = T // C
    n_part = T // TP
    c_per_part = TP // C

    row0 = _iota2((T, 1), 0) == 0

    def shift_mix(x, slot, width, mu):
        prev = carry_ref[slot:slot + 1, 0:width]
        xp = jnp.where(row0, prev, pltpu.roll(x, 1, 0))
        carry_ref[slot:slot + 1, 0:width] = x[T - 1:T, :]
        return x + (xp - x) * mu

    r = shift_mix(r_ref[...], 0, TILE_W, mu_ref[0:1, :])
    k = shift_mix(k_ref[...], 1, TILE_W, mu_ref[1:2, :])
    v = shift_mix(v_ref[...], 2, TILE_W, mu_ref[2:3, :])
    codes = shift_mix(sm_ref[:, 0:LANES], 3, LANES, mu_ref[3:4, 0:LANES])

    w0, a0 = vec_ref[0:1, :], vec_ref[1:2, :]
    k_k, k_a, r_k = vec_ref[2:3, :], vec_ref[3:4, :], vec_ref[4:5, :]
    ln_g, ln_b = vec_ref[5:6, :], vec_ref[6:7, :]

    lane = _iota2((T, LANES), 1)
    codes_t = jnp.where(lane < RW_RANK, jnp.tanh(codes), codes).astype(BF16)
    proj = _dot(codes_t, up_ref[...])
    w = -_softplus(-(w0 + proj[:, 0:TILE_W])) - 0.5
    lw = -jnp.exp(w)
    a = _sigmoid(a0 + proj[:, TILE_W:2 * TILE_W])

    ones64 = _seg_ones(TILE_W, RW_N)
    kkr = k * k_k
    ss = _dot((kkr * kkr).astype(BF16), ones64)
    kk = kkr / jnp.maximum(jnp.sqrt(ss), 1e-12)
    k2 = k * (1.0 + (a - 1.0) * k_a)
    alpha = -kk
    beta = kk * a

    ri_t, ci_t, same_t = _chunk_masks(T, C)
    tri = jnp.where(same_t & (ci_t <= ri_t), 1.0, 0.0).astype(BF16)
    blk = jnp.where(same_t, 1.0, 0.0).astype(BF16)
    cw = _mask_dot_left(tri, lw)
    cwl = _mask_dot_left(blk, lw)
    e_in = jnp.exp(cw)
    e_out = jnp.exp(-cw)
    e_end = jnp.exp(cwl - cw)
    a_t = alpha * jnp.exp(cw - lw)
    r_t = r * e_in
    b_t = beta * e_out
    k_t = k2 * e_out
    b_h = (beta * e_end).astype(BF16)
    k_h = (k2 * e_end).astype(BF16)
    p_tot = jnp.exp(cwl)

    a_tt = a_t.T.astype(BF16)
    r_tt = r_t.T.astype(BF16)
    v_tt = v.T.astype(BF16)

    ri, ci, same = _chunk_masks(TP, C)
    up_strict = same & (ci > ri)
    up_incl = same & (ci >= ri)
    lane_head = _iota2((1, LANES), 1) >> 6
    heads = range(RW_H)
    units = [(p, h) for p in range(n_part) for h in heads]

    def hrows(h):
        return slice(h * RW_N, (h + 1) * RW_N)

    m_ba, m_br, m_k = {}, {}, {}
    for u in units:
        p, h = u
        tok = slice(p * TP, (p + 1) * TP)
        win = slice((h // 2) * LANES, (h // 2 + 1) * LANES)
        hmask = lane_head == (h % 2)
        lhs = jnp.concatenate([jnp.where(hmask, b_t[tok, win], 0.0),
                               jnp.where(hmask, k_t[tok, win], 0.0)], axis=0).astype(BF16)
        rhs = jnp.concatenate([a_tt[win, tok], r_tt[win, tok]], axis=1)
        m = _dot(lhs, rhs)
        m_ba[u] = jnp.where(up_strict, m[0:TP, 0:TP], 0.0)
        m_br[u] = jnp.where(up_incl, m[0:TP, TP:2 * TP], 0.0).astype(BF16)
        m_k[u] = jnp.concatenate([jnp.where(up_strict, m[TP:2 * TP, 0:TP], 0.0),
                                  jnp.where(up_incl, m[TP:2 * TP, TP:2 * TP], 0.0)], axis=1).astype(BF16)
    inv = _block_inverse_all([m_ba[u] for u in units], ri, ci, C)
    tts = {u: x.astype(BF16) for u, x in zip(units, inv)}
    g2 = {(p, h): _dot(v_tt[hrows(h), p * TP:(p + 1) * TP], m_k[(p, h)]) for p, h in units}
    wu = {(p, h): _dot(jnp.concatenate([a_tt[hrows(h), p * TP:(p + 1) * TP],
                                        g2[(p, h)][:, 0:TP].astype(BF16)], axis=0), tts[(p, h)])
          for p, h in units}
    w_t = {u: wu[u][0:RW_N, :].astype(BF16) for u in units}
    kv = [[_dot(v_tt[hrows(h), c * C:(c + 1) * C], k_h[c * C:(c + 1) * C, hrows(h)])
           for c in range(n_chunk)] for h in heads]

    st = [st_ref[hrows(h), :] for h in heads]
    yts = [[] for _ in heads]
    for c in range(n_chunk):
        c0 = c * C
        p = c // c_per_part
        cl = (c % c_per_part) * C
        d1 = [_dot(st[h].astype(BF16),
                   jnp.concatenate([w_t[(p, h)][:, cl:cl + C], r_tt[hrows(h), c0:c0 + C]], axis=1))
              for h in heads]
        ut = [(d1[h][:, 0:C] + wu[(p, h)][RW_N:2 * RW_N, cl:cl + C]).astype(BF16) for h in heads]
        d2 = [_dot(ut[h], jnp.concatenate([m_br[(p, h)][cl:cl + C, cl:cl + C],
                                            b_h[c0:c0 + C, hrows(h)]], axis=1))
              for h in heads]
        for h in heads:
            yts[h].append(d1[h][:, C:2 * C] + g2[(p, h)][:, TP + cl:TP + cl + C] + d2[h][:, 0:C])
            st[h] = st[h] * p_tot[c0:c0 + 1, hrows(h)] + d2[h][:, C:C + RW_N] + kv[h][c]
    yt_heads = []
    for h in heads:
        st_ref[hrows(h), :] = st[h]
        yt = jnp.concatenate(yts[h], axis=1)
        d = yt - jnp.mean(yt, axis=0, keepdims=True)
        yt_heads.append(d * lax.rsqrt(jnp.mean(d * d, axis=0, keepdims=True) + RW_GN_EPS))
    y = jnp.concatenate(yt_heads, axis=0).T * ln_g + ln_b
    bonus = _dot((r * k2 * r_k).astype(BF16), ones64) * v
    o_ref[...] = ((y + bonus) * _silu(z_ref[...])).astype(o_ref.dtype)


def _rwkv_call(u, w, layer, mu, vec, up):
    bsz, seq, d = u.shape
    t, n_sub = _mix_tiling(seq, RW_SUB)
    kern = functools.partial(_rwkv_kernel, t_blk=t // n_sub, n_sub=n_sub, chunk=min(RW_CHUNK, t),
                             pair_t=min(RW_PAIR_T, t))
    return pl.pallas_call(
        kern,
        grid=(bsz, seq // t),
        in_specs=[_u_spec(t, d), _layer_spec(layer, d, w.shape[2]),
                  _layer_spec(layer, 8, TILE_W), _layer_spec(layer, 8, TILE_W),
                  _layer_spec(layer, LANES, 2 * TILE_W)],
        out_specs=pl.BlockSpec((None, t, BRANCH_W), lambda b, i: (b, i, 0)),
        out_shape=jax.ShapeDtypeStruct((bsz, seq, BRANCH_W), BF16),
        scratch_shapes=[pltpu.VMEM((8, TILE_W), F32), pltpu.VMEM((RW_H * RW_N, RW_N), F32)],
        compiler_params=_cparams(("parallel", "arbitrary")),
        name="rwkv7_mix",
    )(u, w, mu, vec, up)


def _gla_recurrence(q, k, v, logg, st_ref, qs_ref, ks_ref, vs_ref, gs_ref, os_ref, *,
                    n_head, dk, t_blk, chunk):
    T, C = t_blk, chunk
    n_chunk = T // C
    dv = LANES

    def window(h):
        if dk == LANES:
            return h * LANES, None
        return (h // 2) * LANES, (_iota2((1, LANES), 1) >> 6) == (h % 2)

    ri, ci, same = _chunk_masks(T, C)
    tri = jnp.where(same & (ci <= ri), 1.0, 0.0).astype(BF16)
    blk = jnp.where(same, 1.0, 0.0).astype(BF16)
    b = _mask_dot_left(tri, logg)
    btot = _mask_dot_left(blk, logg)
    worst = jnp.max(jnp.max(-btot, axis=1, keepdims=True), axis=0, keepdims=True)
    heads = range(n_head)
    chunks = range(n_chunk)
    st_in = [st_ref[h * dv:(h + 1) * dv, :] for h in heads]

    q_t = (q * jnp.exp(b)).astype(BF16)
    k_t = k * jnp.exp(-b)
    k_h = k * jnp.exp(btot - b)
    d_tot = jnp.exp(btot)
    v_tt = v.T.astype(BF16)
    vb = v.astype(BF16)
    causal = same & (ci <= ri)
    qw, ktw, khw = [], [], []
    for h in heads:
        w0, hm = window(h)
        qw.append(q_t[:, w0:w0 + LANES])
        kt_h, kh_h = k_t[:, w0:w0 + LANES], k_h[:, w0:w0 + LANES]
        if hm is not None:
            kt_h, kh_h = jnp.where(hm, kt_h, 0.0), jnp.where(hm, kh_h, 0.0)
        ktw.append(kt_h.astype(BF16))
        khw.append(kh_h.astype(BF16))
    att = [jnp.where(causal, _dot_nt(qw[h], ktw[h]), 0.0).astype(BF16) for h in heads]
    ds = [[_dot(v_tt[h * dv:(h + 1) * dv, c * C:(c + 1) * C], khw[h][c * C:(c + 1) * C, :]) for c in chunks]
          for h in heads]
    sts = []
    for h in heads:
        w0, _ = window(h)
        st = st_in[h]
        row = []
        for c in chunks:
            row.append(st.astype(BF16))
            st = st * d_tot[c * C:c * C + 1, w0:w0 + LANES] + ds[h][c]
        st_ref[h * dv:(h + 1) * dv, :] = st
        sts.append(row)
    for h in heads:
        inter = [_dot_nt(qw[h][c * C:(c + 1) * C, :], sts[h][c]) for c in chunks]
        os_ref[:, h * dv:(h + 1) * dv] = (_dot(att[h], vb[:, h * dv:(h + 1) * dv])
                                          + jnp.concatenate(inter, axis=0))

    @pl.when(worst[0, 0] > SAFE_EXP)
    def _():
        qs_ref[...] = q
        ks_ref[...] = k
        gs_ref[...] = logg
        vs_ref[...] = v
        eye = _iota2((LANES, LANES), 0) == _iota2((LANES, LANES), 1)
        for h in range(n_head):
            w0, hm = window(h)

            def body(t8, st, w0=w0, hm=hm, h=h):
                rows = pl.ds(pl.multiple_of(t8 * 8, 8), 8)
                q8 = qs_ref[rows, w0:w0 + LANES]
                k8 = ks_ref[rows, w0:w0 + LANES]
                e8 = jnp.exp(gs_ref[rows, w0:w0 + LANES])
                v8 = vs_ref[rows, h * dv:(h + 1) * dv]
                if hm is not None:
                    k8 = jnp.where(hm, k8, 0.0)
                    q8 = jnp.where(hm, q8, 0.0)
                o_rows = []
                for j in range(8):
                    v_col = jnp.sum(jnp.where(eye, v8[j:j + 1, :], 0.0), axis=1, keepdims=True)
                    st = st * e8[j:j + 1, :] + v_col * k8[j:j + 1, :]
                    o_col = jnp.sum(st * q8[j:j + 1, :], axis=1, keepdims=True)
                    o_rows.append(jnp.sum(jnp.where(eye, o_col, 0.0), axis=0, keepdims=True))
                os_ref[rows, h * dv:(h + 1) * dv] = jnp.concatenate(o_rows, axis=0)
                return st

            st_ref[h * dv:(h + 1) * dv, :] = lax.fori_loop(0, T // 8, body, st_in[h])


def _head_rms_gate(o, g, z, n_head):
    outs = []
    for h in range(n_head):
        oh = o[:, h * LANES:(h + 1) * LANES]
        outs.append(oh * lax.rsqrt(jnp.mean(oh * oh, axis=-1, keepdims=True) + NORM_EPS))
    return jnp.concatenate(outs, axis=1) * g * _silu(z)


def _gla_kernel(u_ref, w_ref, up_ref, vec_ref, o_ref, st_ref, *scratch, t_blk, n_sub, chunk):
    @pl.when(pl.program_id(1) == 0)
    def _():
        st_ref[...] = jnp.zeros_like(st_ref)

    def tile(s, qk, v, z, code):
        _gla_tile(qk, v, code, z, up_ref, vec_ref, _sub(o_ref, s, t_blk), st_ref, *scratch,
                  t_blk=t_blk, chunk=chunk)

    _run_fused_tiles(tile, (TILE_W, TILE_W, TILE_W, SMALL_W), u_ref, w_ref, n_sub, t_blk)


def _gla_tile(qk, v, code, z, up_ref, vec_ref, o_ref,
              st_ref, qs_ref, ks_ref, vs_ref, gs_ref, os_ref, *, t_blk, chunk):
    hk = GLA_H * GLA_DK
    q = qk[:, 0:hk] * (GLA_DK ** -0.5)
    k = qk[:, hk:2 * hk]
    logits = _dot(code.astype(BF16), up_ref[...]) + vec_ref[0:1, 0:hk]
    logg = _log_sigmoid(logits) / GLA_LOGIT_NORM
    _gla_recurrence(q, k, v, logg, st_ref, qs_ref, ks_ref, vs_ref, gs_ref, os_ref,
                    n_head=GLA_H, dk=GLA_DK, t_blk=t_blk, chunk=chunk)
    o_ref[...] = _head_rms_gate(os_ref[...], vec_ref[1:2, :], z, GLA_H).astype(o_ref.dtype)


def _hgrn_kernel(u_ref, w_ref, vec_ref, o_ref, st_ref, *scratch, t_blk, n_sub, chunk):
    @pl.when(pl.program_id(1) == 0)
    def _():
        st_ref[...] = jnp.zeros_like(st_ref)

    def tile(s, q, f, i, z):
        _hgrn_tile(q, f, i, z, vec_ref, _sub(o_ref, s, t_blk), st_ref, *scratch, t_blk=t_blk, chunk=chunk)

    _run_fused_tiles(tile, (TILE_W,) * 4, u_ref, w_ref, n_sub, t_blk)


def _hgrn_tile(q, f, i, z, vec_ref, o_ref,
               st_ref, qs_ref, ks_ref, vs_ref, gs_ref, os_ref, *, t_blk, chunk):
    lb = vec_ref[0:1, :]
    g = lb + (1.0 - lb) * _sigmoid(f)
    _gla_recurrence(_silu(q), 1.0 - g, i, jnp.log(g), st_ref,
                    qs_ref, ks_ref, vs_ref, gs_ref, os_ref,
                    n_head=HG_H, dk=LANES, t_blk=t_blk, chunk=chunk)
    o_ref[...] = _head_rms_gate(os_ref[...], vec_ref[1:2, :], z, HG_H).astype(o_ref.dtype)


def _gla_scratch(t, key_w):
    return [pltpu.VMEM((4 * LANES, LANES), F32), pltpu.VMEM((t, key_w), F32), pltpu.VMEM((t, key_w), F32),
            pltpu.VMEM((t, BRANCH_W), F32), pltpu.VMEM((t, key_w), F32), pltpu.VMEM((t, BRANCH_W), F32)]


def _gla_call(u, w, layer, up, vec):
    bsz, seq, d = u.shape
    t, n_sub = _mix_tiling(seq, FUSED_SUB)
    kern = functools.partial(_gla_kernel, t_blk=t // n_sub, n_sub=n_sub, chunk=min(GLA_CHUNK, t))
    return pl.pallas_call(
        kern,
        grid=(bsz, seq // t),
        in_specs=[_u_spec(t, d), _layer_spec(layer, d, w.shape[2]),
                  _layer_spec(layer, LANES, GLA_H * GLA_DK), _layer_spec(layer, 8, TILE_W)],
        out_specs=pl.BlockSpec((None, t, BRANCH_W), lambda b, i: (b, i, 0)),
        out_shape=jax.ShapeDtypeStruct((bsz, seq, BRANCH_W), BF16),
        scratch_shapes=_gla_scratch(t // n_sub, GLA_H * GLA_DK),
        compiler_params=_cparams(("parallel", "arbitrary")),
        name="gla_mix",
    )(u, w, up, vec)


def _hgrn_call(u, w, layer, vec):
    bsz, seq, d = u.shape
    t, n_sub = _mix_tiling(seq, FUSED_SUB)
    chunk = HG_CHUNK_FIRST if layer == 0 else HG_CHUNK
    kern = functools.partial(_hgrn_kernel, t_blk=t // n_sub, n_sub=n_sub, chunk=min(chunk, t))
    return pl.pallas_call(
        kern,
        grid=(bsz, seq // t),
        in_specs=[_u_spec(t, d), _layer_spec(layer, d, w.shape[2]), _layer_spec(layer, 8, TILE_W)],
        out_specs=pl.BlockSpec((None, t, BRANCH_W), lambda b, i: (b, i, 0)),
        out_shape=jax.ShapeDtypeStruct((bsz, seq, BRANCH_W), BF16),
        scratch_shapes=_gla_scratch(t // n_sub, TILE_W),
        compiler_params=_cparams(("parallel", "arbitrary")),
        name="hgrn2_mix",
    )(u, w, vec)


def _mlstm_kernel(u_ref, w_ref, conv_ref, vec_ref, o_ref,
                  prev_ref, ct_ref, n_ref, m_ref, *, t_blk, n_sub, chunk):
    @pl.when(pl.program_id(1) == 0)
    def _():
        prev_ref[...] = jnp.zeros_like(prev_ref)
        ct_ref[...] = jnp.zeros_like(ct_ref)
        n_ref[...] = jnp.zeros_like(n_ref)
        m_ref[...] = jnp.full_like(m_ref, STAB_INIT)

    def tile(s, qk, v, z, gates):
        _mlstm_tile(qk, v, gates, z, conv_ref, vec_ref, _sub(o_ref, s, t_blk), prev_ref, ct_ref, n_ref, m_ref,
                    t_blk=t_blk, chunk=chunk)

    _run_fused_tiles(tile, (TILE_W, TILE_W, TILE_W, SMALL_W), u_ref, w_ref, n_sub, t_blk)


def _mlstm_tile(x, v, gates, z, conv_ref, vec_ref, o_ref,
                prev_ref, ct_ref, n_ref, m_ref, *, t_blk, chunk):
    T, C = t_blk, chunk
    n_chunk = T // C
    heads = range(ML_H)

    xx = jnp.concatenate([prev_ref[...], x], axis=0)
    prev_ref[...] = x[T - 8:T, :]
    conv = x * conv_ref[ML_CONV - 1:ML_CONV, :]
    for d in range(1, ML_CONV):
        conv = conv + xx[8 - d:8 - d + T, :] * conv_ref[ML_CONV - 1 - d:ML_CONV - d, :]
    qk = _silu(conv)
    hq = ML_H * ML_DQK
    qb = qk[:, 0:hq].astype(BF16)
    kf = qk[:, hq:2 * hq] * (ML_DQK ** -0.5)
    kb = kf.astype(BF16)
    vb = v.astype(BF16)
    v_tt = v.T.astype(BF16)

    i_col = gates + vec_ref[0:1, 0:LANES]
    lf_col = _log_sigmoid(gates + vec_ref[1:2, 0:LANES])
    ri_t, ci_t, same_t = _chunk_masks(T, C)
    tri = jnp.where(same_t & (ci_t <= ri_t), 1.0, 0.0).astype(BF16)
    cf_col = _mask_dot_left(tri, lf_col)
    cf_row = cf_col.T
    i_row = i_col.T
    li, lf = ML_I_LANE, ML_F_LANE
    causal = _iota2((C, C), 1) <= _iota2((C, C), 0)
    lane_head = _iota2((1, LANES), 1) >> 6

    ct = [ct_ref[h * LANES:(h + 1) * LANES, :] for h in heads]
    n_row = [n_ref[h:h + 1, :] for h in heads]
    m_prev = [m_ref[h:h + 1, 0:1] for h in heads]
    outs = [[] for _ in heads]
    for c in range(n_chunk):
        tok = slice(c * C, (c + 1) * C)
        for h in heads:
            win = slice((h // 2) * LANES, (h // 2 + 1) * LANES)
            hm = lane_head == (h % 2)
            cf_i = cf_col[tok, lf + h:lf + h + 1]
            i_i = i_col[tok, li + h:li + h + 1]
            cf_j = cf_row[lf + h:lf + h + 1, tok]
            i_j = i_row[li + h:li + h + 1, tok]
            log_d = jnp.where(causal, cf_i - cf_j + i_j, -jnp.inf)
            log_inter = cf_i + m_prev[h]
            m_t = jnp.maximum(log_inter, jnp.max(log_d, axis=-1, keepdims=True))
            qh = jnp.where(hm, qb[tok, win], 0.0)
            s = _dot_nt(qh, kb[tok, win]) * jnp.exp(log_d - m_t)
            w_inter = jnp.exp(log_inter - m_t)
            num = (_dot(s.astype(BF16), vb[tok, h * LANES:(h + 1) * LANES])
                   + w_inter * _dot_nt(qh, ct[h].astype(BF16)))
            den = jnp.sum(s, axis=-1, keepdims=True) + w_inter * jnp.sum(
                jnp.where(hm, qk[tok, win], 0.0) * n_row[h], axis=-1, keepdims=True)
            outs[h].append(num / jnp.maximum(jnp.abs(den), jnp.exp(-m_t)))
            m_new = m_t[C - 1:C, :]
            cf_last = cf_i[C - 1:C, :]
            w_carry = jnp.exp(cf_last + m_prev[h] - m_new)
            kw = jnp.where(hm, kf[tok, win], 0.0) * jnp.exp(cf_last - cf_i + i_i - m_new)
            ct[h] = w_carry * ct[h] + _dot(v_tt[h * LANES:(h + 1) * LANES, tok], kw.astype(BF16))
            n_row[h] = w_carry * n_row[h] + jnp.sum(kw, axis=0, keepdims=True)
            m_prev[h] = m_new

    ln = []
    for h in heads:
        ct_ref[h * LANES:(h + 1) * LANES, :] = ct[h]
        n_ref[h:h + 1, :] = n_row[h]
        m_ref[h:h + 1, :] = jnp.broadcast_to(m_prev[h], (1, LANES))
        oh = jnp.concatenate(outs[h], axis=0)
        d = oh - jnp.mean(oh, axis=-1, keepdims=True)
        ln.append(d * lax.rsqrt(jnp.mean(d * d, axis=-1, keepdims=True) + NORM_EPS))
    o_ref[...] = (jnp.concatenate(ln, axis=1) * vec_ref[2:3, :] * _silu(z)).astype(o_ref.dtype)


def _mlstm_call(u, w, layer, conv_w, vec):
    bsz, seq, d = u.shape
    t, n_sub = _mix_tiling(seq, FUSED_SUB)
    kern = functools.partial(_mlstm_kernel, t_blk=t // n_sub, n_sub=n_sub, chunk=min(ML_CHUNK, t))
    return pl.pallas_call(
        kern,
        grid=(bsz, seq // t),
        in_specs=[_u_spec(t, d), _layer_spec(layer, d, w.shape[2]),
                  _layer_spec(layer, 8, TILE_W), _layer_spec(layer, 8, TILE_W)],
        out_specs=pl.BlockSpec((None, t, BRANCH_W), lambda b, i: (b, i, 0)),
        out_shape=jax.ShapeDtypeStruct((bsz, seq, BRANCH_W), BF16),
        scratch_shapes=[pltpu.VMEM((8, TILE_W), F32), pltpu.VMEM((ML_H * LANES, LANES), F32),
                        pltpu.VMEM((8, LANES), F32), pltpu.VMEM((8, LANES), F32)],
        compiler_params=_cparams(("parallel", "arbitrary")),
        name="mlstm_mix",
    )(u, w, conv_w, vec)


def _merge_kernel(x_ref, ya_ref, yb_ref, yc_ref, yd_ref, *rest, final):
    gate_refs = rest[0:N_GATE_TILES]
    wb_ref, wo_ref, res_gate_ref, ng_ref, nsc_ref, nsh_ref = rest[N_GATE_TILES:N_GATE_TILES + 6]
    out_refs = rest[N_GATE_TILES + 6:]
    per_branch = N_GATE_TILES // N_BRANCH
    merged = None
    for m, y_ref in enumerate((ya_ref, yb_ref, yc_ref, yd_ref)):
        br = _dot(y_ref[...], wb_ref[m])
        gates = jnp.concatenate([gate_refs[per_branch * m + j][...] for j in range(per_branch)], axis=1)
        term = gates * br
        merged = term if merged is None else merged + term
    x_new = x_ref[...] + res_gate_ref[...] * _dot(merged.astype(BF16), wo_ref[...])
    ms = jnp.mean(x_new * x_new, axis=-1, keepdims=True)
    normed = x_new * lax.rsqrt(ms + NORM_EPS) * ng_ref[...]
    if final:
        out_refs[0][...] = normed
    else:
        out_refs[0][...] = x_new
        out_refs[1][...] = (normed * (1.0 + nsc_ref[...]) + nsh_ref[...]).astype(BF16)


def _merge_call(x, ys, gates, layer, w_branch, w_out, res_gate, next_g, next_scale, next_shift, final):
    bsz, seq, d = x.shape
    tm = min(seq, 512)
    row = pl.BlockSpec((None, tm, d), lambda b, i: (b, i, 0))
    y_spec = pl.BlockSpec((None, tm, BRANCH_W), lambda b, i: (b, i, 0))
    vec = pl.BlockSpec((None, 1, d), lambda b, i: (b, 0, 0))
    gate_specs = [_tile_spec(j, tm) for j in range(N_GATE_TILES)]
    out_shape = [jax.ShapeDtypeStruct((bsz, seq, d), F32)]
    out_specs = [row]
    if not final:
        out_shape.append(jax.ShapeDtypeStruct((bsz, seq, d), BF16))
        out_specs.append(row)
    return pl.pallas_call(
        functools.partial(_merge_kernel, final=final),
        grid=(bsz, seq // tm),
        in_specs=[row, y_spec, y_spec, y_spec, y_spec] + gate_specs
        + [pl.BlockSpec((None, N_BRANCH, BRANCH_W, d), lambda b, i: (layer, 0, 0, 0)),
           pl.BlockSpec((None, d, d), lambda b, i: (layer, 0, 0)),
           vec, pl.BlockSpec((1, d), lambda b, i: (0, 0)), vec, vec],
        out_specs=out_specs,
        out_shape=out_shape,
        compiler_params=_cparams(("parallel", "parallel")),
        name="merge_out",
    )(x, *ys, *([gates] * N_GATE_TILES), w_branch, w_out, res_gate.reshape(bsz, 1, d),
      next_g.reshape(1, d), next_scale.reshape(bsz, 1, d), next_shift.reshape(bsz, 1, d))


def _pad_last(a, width):
    return jnp.pad(a, [(0, 0)] * (a.ndim - 1) + [(0, width - a.shape[-1])])


def _rows(rows, width, n_rows=8):
    out = [_pad_last(r.astype(F32), width)[:, None, :] for r in rows]
    out.append(jnp.zeros((rows[0].shape[0], n_rows - len(rows), width), F32))
    return jnp.concatenate(out, axis=1)


def _relayout_w_in(w):
    w = w.astype(BF16)
    o = 0

    def take(n):
        nonlocal o
        s = w[:, :, o:o + n]
        o += n
        return s

    rw_r, rw_k, rw_v, rw_wc, rw_ac, rw_z = take(512), take(512), take(512), take(64), take(64), take(512)
    gq, gk, gv, gcode, gz = take(256), take(256), take(512), take(16), take(512)
    mqk, mv, mi, mf, mz = take(512), take(512), take(4), take(4), take(512)
    hq, hf, hi, hz = take(512), take(512), take(512), take(512)
    merge = take(N_BRANCH * D_MODEL)
    lead = jnp.zeros(w.shape[:2] + (ML_I_LANE,), w.dtype)
    blocks = {
        "rw": [rw_r, rw_k, rw_v, rw_z, rw_wc, rw_ac],
        "gla": [gq, gk, gv, gz, _pad_last(gcode, SMALL_W)],
        "ml": [mqk, mv, mz, _pad_last(jnp.concatenate([lead, mi, mf], axis=2), SMALL_W)],
        "hg": [hq, hf, hi, hz],
        "gate": [merge],
    }
    return {name: jnp.concatenate(cols, axis=2) for name, cols in blocks.items()}


def kernel(x, c, norm_g, ada_w, ada_b, w_in, rw_mu, rw_w0, rw_w_up, rw_a0, rw_a_up, rw_k_k, rw_k_a,
           rw_r_k, rw_ln_g, rw_ln_b, gla_gk_up, gla_gk_b, gla_norm_g, ml_conv_w, ml_i_b, ml_f_b,
           ml_norm_g, hg_lb_logits, hg_norm_g, w_branch, w_out, final_g):
    depth = w_in.shape[0]
    d = x.shape[-1]
    lb_p = jax.nn.softmax(hg_lb_logits.astype(F32), axis=0)
    lower_bounds = jnp.cumsum(lb_p, axis=0) - lb_p[0]

    w_cols = _relayout_w_in(w_in)
    rw_mu_rows = _rows([rw_mu[:, 0:512], rw_mu[:, 512:1024], rw_mu[:, 1024:1536], rw_mu[:, 1536:1664]], TILE_W)
    rw_vec = _rows([rw_w0, rw_a0, rw_k_k, rw_k_a, rw_r_k, rw_ln_g, rw_ln_b], TILE_W)
    zeros = jnp.zeros((depth, RW_RANK, BRANCH_W), F32)
    rw_up = jnp.concatenate([jnp.concatenate([rw_w_up, zeros], axis=2),
                             jnp.concatenate([zeros, rw_a_up], axis=2)], axis=1).astype(BF16)
    gla_up = jnp.pad(gla_gk_up, ((0, 0), (0, LANES - GLA_RANK), (0, 0))).astype(BF16)
    gla_vec = _rows([gla_gk_b, gla_norm_g], TILE_W)
    ml_conv = jnp.pad(ml_conv_w.astype(F32), ((0, 0), (0, 8 - ML_CONV), (0, 0)))
    ml_vec = _rows([jnp.pad(ml_i_b, ((0, 0), (ML_I_LANE, 0))),
                    jnp.pad(ml_f_b, ((0, 0), (ML_F_LANE, 0))), ml_norm_g], TILE_W)
    hg_vec = _rows([lower_bounds, hg_norm_g], TILE_W)
    wb = w_branch.astype(BF16)
    wo = w_out.astype(BF16)

    mod = _ada_call(c, ada_w, ada_b)
    u = _norm_call(x, norm_g[0], mod[0, :, d:2 * d], mod[0, :, 0:d])
    for l in range(depth):
        gates = _proj_call(u, w_cols["gate"], l, group=4, rows=2048, gate=True, name="gate_proj")
        y_a = _rwkv_call(u, w_cols["rw"], l, rw_mu_rows, rw_vec, rw_up)
        y_b = _gla_call(u, w_cols["gla"], l, gla_up, gla_vec)
        y_c = _mlstm_call(u, w_cols["ml"], l, ml_conv, ml_vec)
        y_d = _hgrn_call(u, w_cols["hg"], l, hg_vec)
        final = l == depth - 1
        if final:
            next_g, next_scale, next_shift = final_g, mod[l, :, d:2 * d], mod[l, :, 0:d]
        else:
            next_g, next_scale, next_shift = norm_g[l + 1], mod[l + 1, :, d:2 * d], mod[l + 1, :, 0:d]
        res = _merge_call(x, (y_a, y_b, y_c, y_d), gates, l, wb, wo, mod[l, :, 2 * d:3 * d],
                          next_g, next_scale, next_shift, final)
        if final:
            x = res[0]
        else:
            x, u = res
    return x
```

```python
import functools

import jax
import jax.numpy as jnp
import numpy as np
from jax import lax
from jax.experimental import pallas as pl
from jax.experimental.pallas import tpu as pltpu

F32 = jnp.float32
BF16 = jnp.bfloat16

D_MODEL = 1024
N_BRANCH = 4
BRANCH_W = 512
RW_N = 64
RW_H = 8
RW_RANK = 64
RW_GN_EPS = 64e-5
GLA_H = 4
GLA_DK = 64
GLA_RANK = 16
GLA_LOGIT_NORM = 16.0
ML_H = 4
ML_DQK = 64
ML_CONV = 4
STAB_INIT = -1e30
HG_H = 4
NORM_EPS = 1e-6

LANES = 128
TILE_W = 512
MIX_T = 256
RW_SUB = 2
FUSED_SUB = 4
SMALL_W = 128
RW_CHUNK = 64
RW_PAIR_T = 128
GLA_CHUNK = 128
HG_CHUNK_FIRST = 16
HG_CHUNK = 32
ML_CHUNK = 256
SAFE_EXP = 80.0
VMEM_LIMIT = 56 * 1024 * 1024

N_GATE_TILES = N_BRANCH * D_MODEL // TILE_W
ML_I_LANE = 16
ML_F_LANE = 20


def _cparams(sem):
    return pltpu.CompilerParams(dimension_semantics=sem, vmem_limit_bytes=VMEM_LIMIT)


def _sigmoid(x):
    return 1.0 / (1.0 + jnp.exp(-x))


def _silu(x):
    return x * _sigmoid(x)


def _softplus(x):
    return jnp.maximum(x, 0.0) + jnp.log(1.0 + jnp.exp(-jnp.abs(x)))


def _log_sigmoid(x):
    return -_softplus(-x)


def _split2(x):
    hi = x.astype(BF16)
    lo = (x - hi.astype(F32)).astype(BF16)
    return hi, lo


def _dot(a, b):
    return jnp.dot(a, b, preferred_element_type=F32)


def _dot_nt(a, b):
    return lax.dot_general(a, b, (((1,), (1,)), ((), ())), preferred_element_type=F32)


def _mask_dot_left(m01, x):
    hi, lo = _split2(x)
    return _dot(m01, hi) + _dot(m01, lo)


def _iota2(shape, dim):
    return lax.broadcasted_iota(jnp.int32, shape, dim)


def _chunk_masks(t, c):
    sh = int(np.log2(c))
    ri = _iota2((t, t), 0)
    ci = _iota2((t, t), 1)
    same = (ri >> sh) == (ci >> sh)
    return ri, ci, same


def _seg_ones(width, seg):
    sh = int(np.log2(seg))
    ri = _iota2((width, width), 0)
    ci = _iota2((width, width), 1)
    return jnp.where((ri >> sh) == (ci >> sh), 1.0, 0.0).astype(BF16)


def _rms_mod(x, g, scale, shift):
    ms = jnp.mean(x * x, axis=-1, keepdims=True)
    return x * lax.rsqrt(ms + NORM_EPS) * g * (1.0 + scale) + shift


def _ada_kernel(c_ref, w_ref, b_ref, o_ref):
    cond = _silu(c_ref[...]).astype(BF16)
    o_ref[...] = _dot(cond, w_ref[...].astype(BF16)) + b_ref[...]


def _ada_call(c, ada_w, ada_b):
    depth, d, n3 = ada_w.shape
    bsz = c.shape[0]
    tn = 1024
    return pl.pallas_call(
        _ada_kernel,
        grid=(depth, n3 // tn),
        in_specs=[
            pl.BlockSpec((bsz, d), lambda l, j: (0, 0)),
            pl.BlockSpec((None, d, tn), lambda l, j: (l, 0, j)),
            pl.BlockSpec((None, 1, tn), lambda l, j: (l, 0, j)),
        ],
        out_specs=pl.BlockSpec((None, bsz, tn), lambda l, j: (l, 0, j)),
        out_shape=jax.ShapeDtypeStruct((depth, bsz, n3), F32),
        compiler_params=_cparams(("parallel", "parallel")),
        name="ada_mod",
    )(c, ada_w, ada_b.reshape(depth, 1, n3))


def _norm_kernel(x_ref, g_ref, sc_ref, sh_ref, u_ref):
    u_ref[...] = _rms_mod(x_ref[...], g_ref[...], sc_ref[...], sh_ref[...]).astype(u_ref.dtype)


def _norm_call(x, g, scale, shift):
    bsz, seq, d = x.shape
    tm = min(seq, 1024)
    return pl.pallas_call(
        _norm_kernel,
        grid=(bsz, seq // tm),
        in_specs=[pl.BlockSpec((None, tm, d), lambda b, i: (b, i, 0)),
                  pl.BlockSpec((1, d), lambda b, i: (0, 0)),
                  pl.BlockSpec((None, 1, d), lambda b, i: (b, 0, 0)),
                  pl.BlockSpec((None, 1, d), lambda b, i: (b, 0, 0))],
        out_specs=pl.BlockSpec((None, tm, d), lambda b, i: (b, i, 0)),
        out_shape=jax.ShapeDtypeStruct((bsz, seq, d), BF16),
        compiler_params=_cparams(("parallel", "parallel")),
        name="norm_mod",
    )(x, g.reshape(1, d), scale.reshape(bsz, 1, d), shift.reshape(bsz, 1, d))


def _proj_kernel(u_ref, w_ref, o_ref, *, group, gate):
    u = u_ref[...]
    for t in range(group):
        r = _dot(u, w_ref[:, t * TILE_W:(t + 1) * TILE_W])
        if gate:
            r = _sigmoid(r)
        o_ref[t] = r.astype(o_ref.dtype)


def _proj_call(u, w_tiles, layer, *, group, rows, gate, name):
    bsz, seq, d = u.shape
    n_tiles = w_tiles.shape[2] // TILE_W
    tm = min(seq, rows)
    return pl.pallas_call(
        functools.partial(_proj_kernel, group=group, gate=gate),
        grid=(bsz, seq // tm, n_tiles // group),
        in_specs=[pl.BlockSpec((None, tm, d), lambda b, i, j: (b, i, 0)),
                  pl.BlockSpec((None, d, group * TILE_W), lambda b, i, j: (layer, 0, j))],
        out_specs=pl.BlockSpec((group, None, tm, TILE_W), lambda b, i, j: (j, b, i, 0)),
        out_shape=jax.ShapeDtypeStruct((n_tiles, bsz, seq, TILE_W), BF16 if gate else F32),
        compiler_params=_cparams(("parallel", "parallel", "arbitrary")),
        name=name,
    )(u, w_tiles)


def _mix_tiling(seq, want_sub):
    sub_t = min(seq, MIX_T)
    n_sub = want_sub
    while seq % (sub_t * n_sub):
        n_sub //= 2
    return sub_t * n_sub, n_sub


def _sub(ref, s, t):
    return ref.at[pl.ds(s * t, t), :]


def _u_spec(t, d):
    return pl.BlockSpec((None, t, d), lambda b, i: (b, i, 0))


def _run_fused_tiles(tile_fn, widths, u_ref, w_ref, n_sub, t_blk):
    def column_blocks(s):
        blocks, lo = [], 0
        for wd in widths:
            blocks.append(functools.partial(
                lambda lo, wd: _dot(u_ref[pl.ds(s * t_blk, t_blk), :], w_ref[:, lo:lo + wd]), lo, wd))
            lo += wd
        return blocks

    pending = [block() for block in column_blocks(0)]
    for s in range(n_sub):
        current, pending = pending, []
        todo = column_blocks(s + 1) if s + 1 < n_sub else []

        def advance(todo=todo, done=pending):
            if todo:
                done.append(todo.pop(0)())

        tile_fn(s, advance, *current)
        while todo:
            advance()


def _tile_spec(tile_id, t):
    return pl.BlockSpec((None, None, t, TILE_W), lambda b, i: (tile_id, b, i, 0))


def _layer_spec(layer, rows, width):
    return pl.BlockSpec((None, rows, width), lambda b, i: (layer, 0, 0))


def _block_inverse_all(n0s, ri, ci, c):
    eye = jnp.where(ri == ci, 1.0, 0.0)
    lvl1 = (ri >> 1) == (ci >> 1)
    xs = [eye + jnp.where(lvl1, n0, 0.0) for n0 in n0s]
    for lv in range(2, int(np.log2(c)) + 1):
        m = ((ri >> lv) == (ci >> lv)) & ((ri >> (lv - 1)) != (ci >> (lv - 1)))
        xbs = [x.astype(BF16) for x in xs]
        xns = [_dot(xb, jnp.where(m, n0, 0.0).astype(BF16)).astype(BF16) for xb, n0 in zip(xbs, n0s)]
        xs = [x + _dot(xn, xb) for x, xn, xb in zip(xs, xns, xbs)]
    return xs


def _rwkv_kernel(u_ref, w_ref, mu_ref, vec_ref, up_ref, o_ref,
                 carry_ref, st_ref, *, t_blk, n_sub, chunk, pair_t):
    @pl.when(pl.program_id(1) == 0)
    def _():
        carry_ref[...] = jnp.zeros_like(carry_ref)
        st_ref[...] = jnp.zeros_like(st_ref)

    def tile(s, advance, r, k, v, z, codes):
        _rwkv_tile(r, k, v, codes, z, advance, mu_ref, vec_ref, up_ref, _sub(o_ref, s, t_blk), carry_ref, st_ref,
                   t_blk=t_blk, chunk=chunk, pair_t=pair_t)

    _run_fused_tiles(tile, (TILE_W, TILE_W, TILE_W, TILE_W, SMALL_W), u_ref, w_ref, n_sub, t_blk)


def _rwkv_tile(r_ref, k_ref, v_ref, sm_ref, z_ref, advance, mu_ref, vec_ref, up_ref, o_ref,
               carry_ref, st_ref, *, t_blk, chunk, pair_t):
    T, C, TP = t_blk, chunk, pair_t
    n_chunk = T // C
    n_part = T // TP
    c_per_part = TP // C
    for _ in range(5):
        advance()

    row0 = _iota2((T, 1), 0) == 0

    def shift_mix(x, slot, width, mu):
        prev = carry_ref[slot:slot + 1, 0:width]
        xp = jnp.where(row0, prev, pltpu.roll(x, 1, 0))
        carry_ref[slot:slot + 1, 0:width] = x[T - 1:T, :]
        return x + (xp - x) * mu

    r = shift_mix(r_ref[...], 0, TILE_W, mu_ref[0:1, :])
    k = shift_mix(k_ref[...], 1, TILE_W, mu_ref[1:2, :])
    v = shift_mix(v_ref[...], 2, TILE_W, mu_ref[2:3, :])
    codes = shift_mix(sm_ref[:, 0:LANES], 3, LANES, mu_ref[3:4, 0:LANES])

    w0, a0 = vec_ref[0:1, :], vec_ref[1:2, :]
    k_k, k_a, r_k = vec_ref[2:3, :], vec_ref[3:4, :], vec_ref[4:5, :]
    ln_g, ln_b = vec_ref[5:6, :], vec_ref[6:7, :]

    lane = _iota2((T, LANES), 1)
    codes_t = jnp.where(lane < RW_RANK, jnp.tanh(codes), codes).astype(BF16)
    proj = _dot(codes_t, up_ref[...])
    w = -_softplus(-(w0 + proj[:, 0:TILE_W])) - 0.5
    lw = -jnp.exp(w)
    a = _sigmoid(a0 + proj[:, TILE_W:2 * TILE_W])

    ones64 = _seg_ones(TILE_W, RW_N)
    kkr = k * k_k
    ss = _dot((kkr * kkr).astype(BF16), ones64)
    kk = kkr / jnp.maximum(jnp.sqrt(ss), 1e-12)
    k2 = k * (1.0 + (a - 1.0) * k_a)
    alpha = -kk
    beta = kk * a

    ri_t, ci_t, same_t = _chunk_masks(T, C)
    tri = jnp.where(same_t & (ci_t <= ri_t), 1.0, 0.0).astype(BF16)
    blk = jnp.where(same_t, 1.0, 0.0).astype(BF16)
    cw = _mask_dot_left(tri, lw)
    cwl = _mask_dot_left(blk, lw)
    e_in = jnp.exp(cw)
    e_out = jnp.exp(-cw)
    e_end = jnp.exp(cwl - cw)
    a_t = alpha * jnp.exp(cw - lw)
    r_t = r * e_in
    b_t = beta * e_out
    k_t = k2 * e_out
    b_h = (beta * e_end).astype(BF16)
    k_h = (k2 * e_end).astype(BF16)
    p_tot = jnp.exp(cwl)

    a_tt = a_t.T.astype(BF16)
    r_tt = r_t.T.astype(BF16)
    v_tt = v.T.astype(BF16)

    ri, ci, same = _chunk_masks(TP, C)
    up_strict = same & (ci > ri)
    up_incl = same & (ci >= ri)
    lane_head = _iota2((1, LANES), 1) >> 6
    heads = range(RW_H)
    units = [(p, h) for p in range(n_part) for h in heads]

    def hrows(h):
        return slice(h * RW_N, (h + 1) * RW_N)

    m_ba, m_br, m_k = {}, {}, {}
    for u in units:
        p, h = u
        tok = slice(p * TP, (p + 1) * TP)
        win = slice((h // 2) * LANES, (h // 2 + 1) * LANES)
        hmask = lane_head == (h % 2)
        lhs = jnp.concatenate([jnp.where(hmask, b_t[tok, win], 0.0),
                               jnp.where(hmask, k_t[tok, win], 0.0)], axis=0).astype(BF16)
        rhs = jnp.concatenate([a_tt[win, tok], r_tt[win, tok]], axis=1)
        m = _dot(lhs, rhs)
        m_ba[u] = jnp.where(up_strict, m[0:TP, 0:TP], 0.0)
        m_br[u] = jnp.where(up_incl, m[0:TP, TP:2 * TP], 0.0).astype(BF16)
        m_k[u] = jnp.concatenate([jnp.where(up_strict, m[TP:2 * TP, 0:TP], 0.0),
                                  jnp.where(up_incl, m[TP:2 * TP, TP:2 * TP], 0.0)], axis=1).astype(BF16)
    inv = _block_inverse_all([m_ba[u] for u in units], ri, ci, C)
    tts = {u: x.astype(BF16) for u, x in zip(units, inv)}
    g2 = {(p, h): _dot(v_tt[hrows(h), p * TP:(p + 1) * TP], m_k[(p, h)]) for p, h in units}
    wu = {(p, h): _dot(jnp.concatenate([a_tt[hrows(h), p * TP:(p + 1) * TP],
                                        g2[(p, h)][:, 0:TP].astype(BF16)], axis=0), tts[(p, h)])
          for p, h in units}
    w_t = {u: wu[u][0:RW_N, :].astype(BF16) for u in units}
    kv = [[_dot(v_tt[hrows(h), c * C:(c + 1) * C], k_h[c * C:(c + 1) * C, hrows(h)])
           for c in range(n_chunk)] for h in heads]

    st = [st_ref[hrows(h), :] for h in heads]
    yts = [[] for _ in heads]
    for c in range(n_chunk):
        c0 = c * C
        p = c // c_per_part
        cl = (c % c_per_part) * C
        d1 = [_dot(st[h].astype(BF16),
                   jnp.concatenate([w_t[(p, h)][:, cl:cl + C], r_tt[hrows(h), c0:c0 + C]], axis=1))
              for h in heads]
        ut = [(d1[h][:, 0:C] + wu[(p, h)][RW_N:2 * RW_N, cl:cl + C]).astype(BF16) for h in heads]
        d2 = [_dot(ut[h], jnp.concatenate([m_br[(p, h)][cl:cl + C, cl:cl + C],
                                            b_h[c0:c0 + C, hrows(h)]], axis=1))
              for h in heads]
        for h in heads:
            yts[h].append(d1[h][:, C:2 * C] + g2[(p, h)][:, TP + cl:TP + cl + C] + d2[h][:, 0:C])
            st[h] = st[h] * p_tot[c0:c0 + 1, hrows(h)] + d2[h][:, C:C + RW_N] + kv[h][c]
    yt_heads = []
    for h in heads:
        st_ref[hrows(h), :] = st[h]
        yt = jnp.concatenate(yts[h], axis=1)
        d = yt - jnp.mean(yt, axis=0, keepdims=True)
        yt_heads.append(d * lax.rsqrt(jnp.mean(d * d, axis=0, keepdims=True) + RW_GN_EPS))
    y = jnp.concatenate(yt_heads, axis=0).T * ln_g + ln_b
    bonus = _dot((r * k2 * r_k).astype(BF16), ones64) * v
    o_ref[...] = ((y + bonus) * _silu(z_ref[...])).astype(o_ref.dtype)


def _rwkv_call(u, w, layer, mu, vec, up):
    bsz, seq, d = u.shape
    t, n_sub = _mix_tiling(seq, RW_SUB)
    kern = functools.partial(_rwkv_kernel, t_blk=t // n_sub, n_sub=n_sub, chunk=min(RW_CHUNK, t),
                             pair_t=min(RW_PAIR_T, t))
    return pl.pallas_call(
        kern,
        grid=(bsz, seq // t),
        in_specs=[_u_spec(t, d), _layer_spec(layer, d, w.shape[2]),
                  _layer_spec(layer, 8, TILE_W), _layer_spec(layer, 8, TILE_W),
                  _layer_spec(layer, LANES, 2 * TILE_W)],
        out_specs=pl.BlockSpec((None, t, BRANCH_W), lambda b, i: (b, i, 0)),
        out_shape=jax.ShapeDtypeStruct((bsz, seq, BRANCH_W), BF16),
        scratch_shapes=[pltpu.VMEM((8, TILE_W), F32), pltpu.VMEM((RW_H * RW_N, RW_N), F32)],
        compiler_params=_cparams(("parallel", "arbitrary")),
        name="rwkv7_mix",
    )(u, w, mu, vec, up)


def _gla_recurrence(q, k, v, logg, advance, st_ref, qs_ref, ks_ref, vs_ref, gs_ref, os_ref, *,
                    n_head, dk, t_blk, chunk):
    T, C = t_blk, chunk
    n_chunk = T // C
    dv = LANES

    def window(h):
        if dk == LANES:
            return h * LANES, None
        return (h // 2) * LANES, (_iota2((1, LANES), 1) >> 6) == (h % 2)

    ri, ci, same = _chunk_masks(T, C)
    tri = jnp.where(same & (ci <= ri), 1.0, 0.0).astype(BF16)
    blk = jnp.where(same, 1.0, 0.0).astype(BF16)
    b = _mask_dot_left(tri, logg)
    btot = _mask_dot_left(blk, logg)
    worst = jnp.max(jnp.max(-btot, axis=1, keepdims=True), axis=0, keepdims=True)
    heads = range(n_head)
    chunks = range(n_chunk)
    st_in = [st_ref[h * dv:(h + 1) * dv, :] for h in heads]

    advance()
    q_t = (q * jnp.exp(b)).astype(BF16)
    k_t = k * jnp.exp(-b)
    k_h = k * jnp.exp(btot - b)
    d_tot = jnp.exp(btot)
    v_tt = v.T.astype(BF16)
    vb = v.astype(BF16)
    causal = same & (ci <= ri)
    qw, ktw, khw = [], [], []
    for h in heads:
        w0, hm = window(h)
        qw.append(q_t[:, w0:w0 + LANES])
        kt_h, kh_h = k_t[:, w0:w0 + LANES], k_h[:, w0:w0 + LANES]
        if hm is not None:
            kt_h, kh_h = jnp.where(hm, kt_h, 0.0), jnp.where(hm, kh_h, 0.0)
        ktw.append(kt_h.astype(BF16))
        khw.append(kh_h.astype(BF16))
    advance()
    att = [jnp.where(causal, _dot_nt(qw[h], ktw[h]), 0.0).astype(BF16) for h in heads]
    advance()
    ds = [[_dot(v_tt[h * dv:(h + 1) * dv, c * C:(c + 1) * C], khw[h][c * C:(c + 1) * C, :]) for c in chunks]
          for h in heads]
    advance()
    sts = []
    for h in heads:
        w0, _ = window(h)
        st = st_in[h]
        row = []
        for c in chunks:
            row.append(st.astype(BF16))
            st = st * d_tot[c * C:c * C + 1, w0:w0 + LANES] + ds[h][c]
        st_ref[h * dv:(h + 1) * dv, :] = st
        sts.append(row)
    for h in heads:
        inter = [_dot_nt(qw[h][c * C:(c + 1) * C, :], sts[h][c]) for c in chunks]
        os_ref[:, h * dv:(h + 1) * dv] = (_dot(att[h], vb[:, h * dv:(h + 1) * dv])
                                          + jnp.concatenate(inter, axis=0))

    @pl.when(worst[0, 0] > SAFE_EXP)
    def _():
        qs_ref[...] = q
        ks_ref[...] = k
        gs_ref[...] = logg
        vs_ref[...] = v
        eye = _iota2((LANES, LANES), 0) == _iota2((LANES, LANES), 1)
        for h in range(n_head):
            w0, hm = window(h)

            def body(t8, st, w0=w0, hm=hm, h=h):
                rows = pl.ds(pl.multiple_of(t8 * 8, 8), 8)
                q8 = qs_ref[rows, w0:w0 + LANES]
                k8 = ks_ref[rows, w0:w0 + LANES]
                e8 = jnp.exp(gs_ref[rows, w0:w0 + LANES])
                v8 = vs_ref[rows, h * dv:(h + 1) * dv]
                if hm is not None:
                    k8 = jnp.where(hm, k8, 0.0)
                    q8 = jnp.where(hm, q8, 0.0)
                o_rows = []
                for j in range(8):
                    v_col = jnp.sum(jnp.where(eye, v8[j:j + 1, :], 0.0), axis=1, keepdims=True)
                    st = st * e8[j:j + 1, :] + v_col * k8[j:j + 1, :]
                    o_col = jnp.sum(st * q8[j:j + 1, :], axis=1, keepdims=True)
                    o_rows.append(jnp.sum(jnp.where(eye, o_col, 0.0), axis=0, keepdims=True))
                os_ref[rows, h * dv:(h + 1) * dv] = jnp.concatenate(o_rows, axis=0)
                return st

            st_ref[h * dv:(h + 1) * dv, :] = lax.fori_loop(0, T // 8, body, st_in[h])


def _head_rms_gate(o, g, z, n_head):
    outs = []
    for h in range(n_head):
        oh = o[:, h * LANES:(h + 1) * LANES]
        outs.append(oh * lax.rsqrt(jnp.mean(oh * oh, axis=-1, keepdims=True) + NORM_EPS))
    return jnp.concatenate(outs, axis=1) * g * _silu(z)


def _gla_kernel(u_ref, w_ref, up_ref, vec_ref, o_ref, st_ref, *scratch, t_blk, n_sub, chunk):
    @pl.when(pl.program_id(1) == 0)
    def _():
        st_ref[...] = jnp.zeros_like(st_ref)

    def tile(s, advance, qk, v, z, code):
        _gla_tile(qk, v, code, z, advance, up_ref, vec_ref, _sub(o_ref, s, t_blk), st_ref, *scratch,
                  t_blk=t_blk, chunk=chunk)

    _run_fused_tiles(tile, (TILE_W, TILE_W, TILE_W, SMALL_W), u_ref, w_ref, n_sub, t_blk)


def _gla_tile(qk, v, code, z, advance, up_ref, vec_ref, o_ref,
              st_ref, qs_ref, ks_ref, vs_ref, gs_ref, os_ref, *, t_blk, chunk):
    hk = GLA_H * GLA_DK
    q = qk[:, 0:hk] * (GLA_DK ** -0.5)
    k = qk[:, hk:2 * hk]
    logits = _dot(code.astype(BF16), up_ref[...]) + vec_ref[0:1, 0:hk]
    logg = _log_sigmoid(logits) / GLA_LOGIT_NORM
    _gla_recurrence(q, k, v, logg, advance, st_ref, qs_ref, ks_ref, vs_ref, gs_ref, os_ref,
                    n_head=GLA_H, dk=GLA_DK, t_blk=t_blk, chunk=chunk)
    o_ref[...] = _head_rms_gate(os_ref[...], vec_ref[1:2, :], z, GLA_H).astype(o_ref.dtype)


def _hgrn_kernel(u_ref, w_ref, vec_ref, o_ref, st_ref, *scratch, t_blk, n_sub, chunk):
    @pl.when(pl.program_id(1) == 0)
    def _():
        st_ref[...] = jnp.zeros_like(st_ref)

    def tile(s, advance, q, f, i, z):
        _hgrn_tile(q, f, i, z, advance, vec_ref, _sub(o_ref, s, t_blk), st_ref, *scratch,
                   t_blk=t_blk, chunk=chunk)

    _run_fused_tiles(tile, (TILE_W,) * 4, u_ref, w_ref, n_sub, t_blk)


def _hgrn_tile(q, f, i, z, advance, vec_ref, o_ref,
               st_ref, qs_ref, ks_ref, vs_ref, gs_ref, os_ref, *, t_blk, chunk):
    lb = vec_ref[0:1, :]
    g = lb + (1.0 - lb) * _sigmoid(f)
    _gla_recurrence(_silu(q), 1.0 - g, i, jnp.log(g), advance, st_ref,
                    qs_ref, ks_ref, vs_ref, gs_ref, os_ref,
                    n_head=HG_H, dk=LANES, t_blk=t_blk, chunk=chunk)
    o_ref[...] = _head_rms_gate(os_ref[...], vec_ref[1:2, :], z, HG_H).astype(o_ref.dtype)


def _gla_scratch(t, key_w):
    return [pltpu.VMEM((4 * LANES, LANES), F32), pltpu.VMEM((t, key_w), F32), pltpu.VMEM((t, key_w), F32),
            pltpu.VMEM((t, BRANCH_W), F32), pltpu.VMEM((t, key_w), F32), pltpu.VMEM((t, BRANCH_W), F32)]


def _gla_call(u, w, layer, up, vec):
    bsz, seq, d = u.shape
    t, n_sub = _mix_tiling(seq, FUSED_SUB)
    kern = functools.partial(_gla_kernel, t_blk=t // n_sub, n_sub=n_sub, chunk=min(GLA_CHUNK, t))
    return pl.pallas_call(
        kern,
        grid=(bsz, seq // t),
        in_specs=[_u_spec(t, d), _layer_spec(layer, d, w.shape[2]),
                  _layer_spec(layer, LANES, GLA_H * GLA_DK), _layer_spec(layer, 8, TILE_W)],
        out_specs=pl.BlockSpec((None, t, BRANCH_W), lambda b, i: (b, i, 0)),
        out_shape=jax.ShapeDtypeStruct((bsz, seq, BRANCH_W), BF16),
        scratch_shapes=_gla_scratch(t // n_sub, GLA_H * GLA_DK),
        compiler_params=_cparams(("parallel", "arbitrary")),
        name="gla_mix",
    )(u, w, up, vec)


def _hgrn_call(u, w, layer, vec):
    bsz, seq, d = u.shape
    t, n_sub = _mix_tiling(seq, FUSED_SUB)
    chunk = HG_CHUNK_FIRST if layer == 0 else HG_CHUNK
    kern = functools.partial(_hgrn_kernel, t_blk=t // n_sub, n_sub=n_sub, chunk=min(chunk, t))
    return pl.pallas_call(
        kern,
        grid=(bsz, seq // t),
        in_specs=[_u_spec(t, d), _layer_spec(layer, d, w.shape[2]), _layer_spec(layer, 8, TILE_W)],
        out_specs=pl.BlockSpec((None, t, BRANCH_W), lambda b, i: (b, i, 0)),
        out_shape=jax.ShapeDtypeStruct((bsz, seq, BRANCH_W), BF16),
        scratch_shapes=_gla_scratch(t // n_sub, TILE_W),
        compiler_params=_cparams(("parallel", "arbitrary")),
        name="hgrn2_mix",
    )(u, w, vec)


def _mlstm_kernel(u_ref, w_ref, conv_ref, vec_ref, o_ref,
                  prev_ref, ct_ref, n_ref, m_ref, *, t_blk, n_sub, chunk):
    @pl.when(pl.program_id(1) == 0)
    def _():
        prev_ref[...] = jnp.zeros_like(prev_ref)
        ct_ref[...] = jnp.zeros_like(ct_ref)
        n_ref[...] = jnp.zeros_like(n_ref)
        m_ref[...] = jnp.full_like(m_ref, STAB_INIT)

    def tile(s, advance, qk, v, z, gates):
        _mlstm_tile(qk, v, gates, z, advance, conv_ref, vec_ref, _sub(o_ref, s, t_blk), prev_ref, ct_ref, n_ref,
                    m_ref, t_blk=t_blk, chunk=chunk)

    _run_fused_tiles(tile, (TILE_W, TILE_W, TILE_W, SMALL_W), u_ref, w_ref, n_sub, t_blk)


def _mlstm_tile(x, v, gates, z, advance, conv_ref, vec_ref, o_ref,
                prev_ref, ct_ref, n_ref, m_ref, *, t_blk, chunk):
    T, C = t_blk, chunk
    n_chunk = T // C
    heads = range(ML_H)

    xx = jnp.concatenate([prev_ref[...], x], axis=0)
    prev_ref[...] = x[T - 8:T, :]
    conv = x * conv_ref[ML_CONV - 1:ML_CONV, :]
    for d in range(1, ML_CONV):
        conv = conv + xx[8 - d:8 - d + T, :] * conv_ref[ML_CONV - 1 - d:ML_CONV - d, :]
    qk = _silu(conv)
    hq = ML_H * ML_DQK
    qb = qk[:, 0:hq].astype(BF16)
    kf = qk[:, hq:2 * hq] * (ML_DQK ** -0.5)
    kb = kf.astype(BF16)
    vb = v.astype(BF16)
    v_tt = v.T.astype(BF16)

    i_col = gates + vec_ref[0:1, 0:LANES]
    lf_col = _log_sigmoid(gates + vec_ref[1:2, 0:LANES])
    ri_t, ci_t, same_t = _chunk_masks(T, C)
    tri = jnp.where(same_t & (ci_t <= ri_t), 1.0, 0.0).astype(BF16)
    cf_col = _mask_dot_left(tri, lf_col)
    cf_row = cf_col.T
    i_row = i_col.T
    li, lf = ML_I_LANE, ML_F_LANE
    causal = _iota2((C, C), 1) <= _iota2((C, C), 0)
    lane_head = _iota2((1, LANES), 1) >> 6

    ct = [ct_ref[h * LANES:(h + 1) * LANES, :] for h in heads]
    n_row = [n_ref[h:h + 1, :] for h in heads]
    m_prev = [m_ref[h:h + 1, 0:1] for h in heads]
    outs = [[] for _ in heads]
    for c in range(n_chunk):
        tok = slice(c * C, (c + 1) * C)
        for h in heads:
            advance()
            win = slice((h // 2) * LANES, (h // 2 + 1) * LANES)
            hm = lane_head == (h % 2)
            cf_i = cf_col[tok, lf + h:lf + h + 1]
            i_i = i_col[tok, li + h:li + h + 1]
            cf_j = cf_row[lf + h:lf + h + 1, tok]
            i_j = i_row[li + h:li + h + 1, tok]
            log_d = jnp.where(causal, cf_i - cf_j + i_j, -jnp.inf)
            log_inter = cf_i + m_prev[h]
            m_t = jnp.maximum(log_inter, jnp.max(log_d, axis=-1, keepdims=True))
            qh = jnp.where(hm, qb[tok, win], 0.0)
            s = _dot_nt(qh, kb[tok, win]) * jnp.exp(log_d - m_t)
            w_inter = jnp.exp(log_inter - m_t)
            num = (_dot(s.astype(BF16), vb[tok, h * LANES:(h + 1) * LANES])
                   + w_inter * _dot_nt(qh, ct[h].astype(BF16)))
            den = jnp.sum(s, axis=-1, keepdims=True) + w_inter * jnp.sum(
                jnp.where(hm, qk[tok, win], 0.0) * n_row[h], axis=-1, keepdims=True)
            outs[h].append(num / jnp.maximum(jnp.abs(den), jnp.exp(-m_t)))
            m_new = m_t[C - 1:C, :]
            cf_last = cf_i[C - 1:C, :]
            w_carry = jnp.exp(cf_last + m_prev[h] - m_new)
            kw = jnp.where(hm, kf[tok, win], 0.0) * jnp.exp(cf_last - cf_i + i_i - m_new)
            ct[h] = w_carry * ct[h] + _dot(v_tt[h * LANES:(h + 1) * LANES, tok], kw.astype(BF16))
            n_row[h] = w_carry * n_row[h] + jnp.sum(kw, axis=0, keepdims=True)
            m_prev[h] = m_new

    ln = []
    for h in heads:
        ct_ref[h * LANES:(h + 1) * LANES, :] = ct[h]
        n_ref[h:h + 1, :] = n_row[h]
        m_ref[h:h + 1, :] = jnp.broadcast_to(m_prev[h], (1, LANES))
        oh = jnp.concatenate(outs[h], axis=0)
        d = oh - jnp.mean(oh, axis=-1, keepdims=True)
        ln.append(d * lax.rsqrt(jnp.mean(d * d, axis=-1, keepdims=True) + NORM_EPS))
    o_ref[...] = (jnp.concatenate(ln, axis=1) * vec_ref[2:3, :] * _silu(z)).astype(o_ref.dtype)


def _mlstm_call(u, w, layer, conv_w, vec):
    bsz, seq, d = u.shape
    t, n_sub = _mix_tiling(seq, FUSED_SUB)
    kern = functools.partial(_mlstm_kernel, t_blk=t // n_sub, n_sub=n_sub, chunk=min(ML_CHUNK, t))
    return pl.pallas_call(
        kern,
        grid=(bsz, seq // t),
        in_specs=[_u_spec(t, d), _layer_spec(layer, d, w.shape[2]),
                  _layer_spec(layer, 8, TILE_W), _layer_spec(layer, 8, TILE_W)],
        out_specs=pl.BlockSpec((None, t, BRANCH_W), lambda b, i: (b, i, 0)),
        out_shape=jax.ShapeDtypeStruct((bsz, seq, BRANCH_W), BF16),
        scratch_shapes=[pltpu.VMEM((8, TILE_W), F32), pltpu.VMEM((ML_H * LANES, LANES), F32),
                        pltpu.VMEM((8, LANES), F32), pltpu.VMEM((8, LANES), F32)],
        compiler_params=_cparams(("parallel", "arbitrary")),
        name="mlstm_mix",
    )(u, w, conv_w, vec)


def _merge_kernel(x_ref, ya_ref, yb_ref, yc_ref, yd_ref, *rest, final):
    gate_refs = rest[0:N_GATE_TILES]
    wb_ref, wo_ref, res_gate_ref, ng_ref, nsc_ref, nsh_ref = rest[N_GATE_TILES:N_GATE_TILES + 6]
    out_refs = rest[N_GATE_TILES + 6:]
    per_branch = N_GATE_TILES // N_BRANCH
    merged = None
    for m, y_ref in enumerate((ya_ref, yb_ref, yc_ref, yd_ref)):
        br = _dot(y_ref[...], wb_ref[m])
        gates = jnp.concatenate([gate_refs[per_branch * m + j][...] for j in range(per_branch)], axis=1)
        term = gates * br
        merged = term if merged is None else merged + term
    x_new = x_ref[...] + res_gate_ref[...] * _dot(merged.astype(BF16), wo_ref[...])
    ms = jnp.mean(x_new * x_new, axis=-1, keepdims=True)
    normed = x_new * lax.rsqrt(ms + NORM_EPS) * ng_ref[...]
    if final:
        out_refs[0][...] = normed
    else:
        out_refs[0][...] = x_new
        out_refs[1][...] = (normed * (1.0 + nsc_ref[...]) + nsh_ref[...]).astype(BF16)


def _merge_call(x, ys, gates, layer, w_branch, w_out, res_gate, next_g, next_scale, next_shift, final):
    bsz, seq, d = x.shape
    tm = min(seq, 512)
    row = pl.BlockSpec((None, tm, d), lambda b, i: (b, i, 0))
    y_spec = pl.BlockSpec((None, tm, BRANCH_W), lambda b, i: (b, i, 0))
    vec = pl.BlockSpec((None, 1, d), lambda b, i: (b, 0, 0))
    gate_specs = [_tile_spec(j, tm) for j in range(N_GATE_TILES)]
    out_shape = [jax.ShapeDtypeStruct((bsz, seq, d), F32)]
    out_specs = [row]
    if not final:
        out_shape.append(jax.ShapeDtypeStruct((bsz, seq, d), BF16))
        out_specs.append(row)
    return pl.pallas_call(
        functools.partial(_merge_kernel, final=final),
        grid=(bsz, seq // tm),
        in_specs=[row, y_spec, y_spec, y_spec, y_spec] + gate_specs
        + [pl.BlockSpec((None, N_BRANCH, BRANCH_W, d), lambda b, i: (layer, 0, 0, 0)),
           pl.BlockSpec((None, d, d), lambda b, i: (layer, 0, 0)),
           vec, pl.BlockSpec((1, d), lambda b, i: (0, 0)), vec, vec],
        out_specs=out_specs,
        out_shape=out_shape,
        compiler_params=_cparams(("parallel", "parallel")),
        name="merge_out",
    )(x, *ys, *([gates] * N_GATE_TILES), w_branch, w_out, res_gate.reshape(bsz, 1, d),
      next_g.reshape(1, d), next_scale.reshape(bsz, 1, d), next_shift.reshape(bsz, 1, d))


def _pad_last(a, width):
    return jnp.pad(a, [(0, 0)] * (a.ndim - 1) + [(0, width - a.shape[-1])])


def _rows(rows, width, n_rows=8):
    out = [_pad_last(r.astype(F32), width)[:, None, :] for r in rows]
    out.append(jnp.zeros((rows[0].shape[0], n_rows - len(rows), width), F32))
    return jnp.concatenate(out, axis=1)


def _relayout_w_in(w):
    w = w.astype(BF16)
    o = 0

    def take(n):
        nonlocal o
        s = w[:, :, o:o + n]
        o += n
        return s

    rw_r, rw_k, rw_v, rw_wc, rw_ac, rw_z = take(512), take(512), take(512), take(64), take(64), take(512)
    gq, gk, gv, gcode, gz = take(256), take(256), take(512), take(16), take(512)
    mqk, mv, mi, mf, mz = take(512), take(512), take(4), take(4), take(512)
    hq, hf, hi, hz = take(512), take(512), take(512), take(512)
    merge = take(N_BRANCH * D_MODEL)
    lead = jnp.zeros(w.shape[:2] + (ML_I_LANE,), w.dtype)
    blocks = {
        "rw": [rw_r, rw_k, rw_v, rw_z, rw_wc, rw_ac],
        "gla": [gq, gk, gv, gz, _pad_last(gcode, SMALL_W)],
        "ml": [mqk, mv, mz, _pad_last(jnp.concatenate([lead, mi, mf], axis=2), SMALL_W)],
        "hg": [hq, hf, hi, hz],
        "gate": [merge],
    }
    return {name: jnp.concatenate(cols, axis=2) for name, cols in blocks.items()}


def kernel(x, c, norm_g, ada_w, ada_b, w_in, rw_mu, rw_w0, rw_w_up, rw_a0, rw_a_up, rw_k_k, rw_k_a,
           rw_r_k, rw_ln_g, rw_ln_b, gla_gk_up, gla_gk_b, gla_norm_g, ml_conv_w, ml_i_b, ml_f_b,
           ml_norm_g, hg_lb_logits, hg_norm_g, w_branch, w_out, final_g):
    depth = w_in.shape[0]
    d = x.shape[-1]
    lb_p = jax.nn.softmax(hg_lb_logits.astype(F32), axis=0)
    lower_bounds = jnp.cumsum(lb_p, axis=0) - lb_p[0]

    w_cols = _relayout_w_in(w_in)
    rw_mu_rows = _rows([rw_mu[:, 0:512], rw_mu[:, 512:1024], rw_mu[:, 1024:1536], rw_mu[:, 1536:1664]], TILE_W)
    rw_vec = _rows([rw_w0, rw_a0, rw_k_k, rw_k_a, rw_r_k, rw_ln_g, rw_ln_b], TILE_W)
    zeros = jnp.zeros((depth, RW_RANK, BRANCH_W), F32)
    rw_up = jnp.concatenate([jnp.concatenate([rw_w_up, zeros], axis=2),
                             jnp.concatenate([zeros, rw_a_up], axis=2)], axis=1).astype(BF16)
    gla_up = jnp.pad(gla_gk_up, ((0, 0), (0, LANES - GLA_RANK), (0, 0))).astype(BF16)
    gla_vec = _rows([gla_gk_b, gla_norm_g], TILE_W)
    ml_conv = jnp.pad(ml_conv_w.astype(F32), ((0, 0), (0, 8 - ML_CONV), (0, 0)))
    ml_vec = _rows([jnp.pad(ml_i_b, ((0, 0), (ML_I_LANE, 0))),
                    jnp.pad(ml_f_b, ((0, 0), (ML_F_LANE, 0))), ml_norm_g], TILE_W)
    hg_vec = _rows([lower_bounds, hg_norm_g], TILE_W)
    wb = w_branch.astype(BF16)
    wo = w_out.astype(BF16)

    mod = _ada_call(c, ada_w, ada_b)
    u = _norm_call(x, norm_g[0], mod[0, :, d:2 * d], mod[0, :, 0:d])
    for l in range(depth):
        gates = _proj_call(u, w_cols["gate"], l, group=4, rows=2048, gate=True, name="gate_proj")
        y_a = _rwkv_call(u, w_cols["rw"], l, rw_mu_rows, rw_vec, rw_up)
        y_b = _gla_call(u, w_cols["gla"], l, gla_up, gla_vec)
        y_c = _mlstm_call(u, w_cols["ml"], l, ml_conv, ml_vec)
        y_d = _hgrn_call(u, w_cols["hg"], l, hg_vec)
        final = l == depth - 1
        if final:
            next_g, next_scale, next_shift = final_g, mod[l, :, d:2 * d], mod[l, :, 0:d]
        else:
            next_g, next_scale, next_shift = norm_g[l + 1], mod[l + 1, :, d:2 * d], mod[l + 1, :, 0:d]
        res = _merge_call(x, (y_a, y_b, y_c, y_d), gates, l, wb, wo, mod[l, :, 2 * d:3 * d],
                          next_g, next_scale, next_shift, final)
        if final:
            x = res[0]
        else:
            x, u = res
    return x
```

```python
import functools

import jax
import jax.numpy as jnp
import numpy as np
from jax import lax
from jax.experimental import pallas as pl
from jax.experimental.pallas import tpu as pltpu

F32 = jnp.float32
BF16 = jnp.bfloat16

D_MODEL = 1024
N_BRANCH = 4
BRANCH_W = 512
RW_N = 64
RW_H = 8
RW_RANK = 64
RW_GN_EPS = 64e-5
GLA_H = 4
GLA_DK = 64
GLA_RANK = 16
GLA_LOGIT_NORM = 16.0
ML_H = 4
ML_DQK = 64
ML_CONV = 4
STAB_INIT = -1e30
HG_H = 4
NORM_EPS = 1e-6

LANES = 128
TILE_W = 512
MIX_T = 256
RW_SUB = 2
FUSED_SUB = 4
SMALL_W = 128
RW_CHUNK = 64
RW_PAIR_T = 128
GLA_CHUNK = 128
HG_CHUNK_FIRST = 16
HG_CHUNK = 32
ML_CHUNK = 256
SAFE_EXP = 80.0
VMEM_LIMIT = 56 * 1024 * 1024

N_GATE_TILES = N_BRANCH * D_MODEL // TILE_W
ML_I_LANE = 16
ML_F_LANE = 20


def _cparams(sem):
    return pltpu.CompilerParams(dimension_semantics=sem, vmem_limit_bytes=VMEM_LIMIT)


def _sigmoid(x):
    return 1.0 / (1.0 + jnp.exp(-x))


def _silu(x):
    return x * _sigmoid(x)


def _softplus(x):
    return jnp.maximum(x, 0.0) + jnp.log(1.0 + jnp.exp(-jnp.abs(x)))


def _log_sigmoid(x):
    return -_softplus(-x)


def _split2(x):
    hi = x.astype(BF16)
    lo = (x - hi.astype(F32)).astype(BF16)
    return hi, lo


def _dot(a, b):
    return jnp.dot(a, b, preferred_element_type=F32)


def _dot_nt(a, b):
    return lax.dot_general(a, b, (((1,), (1,)), ((), ())), preferred_element_type=F32)


def _mask_dot_left(m01, x):
    hi, lo = _split2(x)
    return _dot(m01, hi) + _dot(m01, lo)


def _iota2(shape, dim):
    return lax.broadcasted_iota(jnp.int32, shape, dim)


def _chunk_masks(t, c):
    sh = int(np.log2(c))
    ri = _iota2((t, t), 0)
    ci = _iota2((t, t), 1)
    same = (ri >> sh) == (ci >> sh)
    return ri, ci, same


def _seg_ones(width, seg):
    sh = int(np.log2(seg))
    ri = _iota2((width, width), 0)
    ci = _iota2((width, width), 1)
    return jnp.where((ri >> sh) == (ci >> sh), 1.0, 0.0).astype(BF16)


def _rms_mod(x, g, scale, shift):
    ms = jnp.mean(x * x, axis=-1, keepdims=True)
    return x * lax.rsqrt(ms + NORM_EPS) * g * (1.0 + scale) + shift


def _ada_kernel(c_ref, w_ref, b_ref, o_ref):
    cond = _silu(c_ref[...]).astype(BF16)
    o_ref[...] = _dot(cond, w_ref[...].astype(BF16)) + b_ref[...]


def _ada_call(c, ada_w, ada_b):
    depth, d, n3 = ada_w.shape
    bsz = c.shape[0]
    tn = 1024
    return pl.pallas_call(
        _ada_kernel,
        grid=(depth, n3 // tn),
        in_specs=[
            pl.BlockSpec((bsz, d), lambda l, j: (0, 0)),
            pl.BlockSpec((None, d, tn), lambda l, j: (l, 0, j)),
            pl.BlockSpec((None, 1, tn), lambda l, j: (l, 0, j)),
        ],
        out_specs=pl.BlockSpec((None, bsz, tn), lambda l, j: (l, 0, j)),
        out_shape=jax.ShapeDtypeStruct((depth, bsz, n3), F32),
        compiler_params=_cparams(("parallel", "parallel")),
        name="ada_mod",
    )(c, ada_w, ada_b.reshape(depth, 1, n3))


def _norm_kernel(x_ref, g_ref, sc_ref, sh_ref, u_ref):
    u_ref[...] = _rms_mod(x_ref[...], g_ref[...], sc_ref[...], sh_ref[...]).astype(u_ref.dtype)


def _norm_call(x, g, scale, shift):
    bsz, seq, d = x.shape
    tm = min(seq, 1024)
    return pl.pallas_call(
        _norm_kernel,
        grid=(bsz, seq // tm),
        in_specs=[pl.BlockSpec((None, tm, d), lambda b, i: (b, i, 0)),
                  pl.BlockSpec((1, d), lambda b, i: (0, 0)),
                  pl.BlockSpec((None, 1, d), lambda b, i: (b, 0, 0)),
                  pl.BlockSpec((None, 1, d), lambda b, i: (b, 0, 0))],
        out_specs=pl.BlockSpec((None, tm, d), lambda b, i: (b, i, 0)),
        out_shape=jax.ShapeDtypeStruct((bsz, seq, d), BF16),
        compiler_params=_cparams(("parallel", "parallel")),
        name="norm_mod",
    )(x, g.reshape(1, d), scale.reshape(bsz, 1, d), shift.reshape(bsz, 1, d))


def _proj_kernel(u_ref, w_ref, o_ref, *, group, gate):
    u = u_ref[...]
    for t in range(group):
        r = _dot(u, w_ref[:, t * TILE_W:(t + 1) * TILE_W])
        if gate:
            r = _sigmoid(r)
        o_ref[t] = r.astype(o_ref.dtype)


def _proj_call(u, w_tiles, layer, *, group, rows, gate, name):
    bsz, seq, d = u.shape
    n_tiles = w_tiles.shape[2] // TILE_W
    tm = min(seq, rows)
    return pl.pallas_call(
        functools.partial(_proj_kernel, group=group, gate=gate),
        grid=(bsz, seq // tm, n_tiles // group),
        in_specs=[pl.BlockSpec((None, tm, d), lambda b, i, j: (b, i, 0)),
                  pl.BlockSpec((None, d, group * TILE_W), lambda b, i, j: (layer, 0, j))],
        out_specs=pl.BlockSpec((group, None, tm, TILE_W), lambda b, i, j: (j, b, i, 0)),
        out_shape=jax.ShapeDtypeStruct((n_tiles, bsz, seq, TILE_W), BF16 if gate else F32),
        compiler_params=_cparams(("parallel", "parallel", "arbitrary")),
        name=name,
    )(u, w_tiles)


def _mix_tiling(seq, want_sub):
    sub_t = min(seq, MIX_T)
    n_sub = want_sub
    while seq % (sub_t * n_sub):
        n_sub //= 2
    return sub_t * n_sub, n_sub


def _sub(ref, s, t):
    return ref.at[pl.ds(s * t, t), :]


def _u_spec(t, d):
    return pl.BlockSpec((None, t, d), lambda b, i: (b, i, 0))


def _run_fused_tiles(tile_fn, widths, u_ref, w_ref, n_sub, t_blk):
    def column_blocks(s):
        blocks, lo = [], 0
        for wd in widths:
            blocks.append(functools.partial(
                lambda lo, wd: _dot(u_ref[pl.ds(s * t_blk, t_blk), :], w_ref[:, lo:lo + wd]), lo, wd))
            lo += wd
        return blocks

    pending = [block() for block in column_blocks(0)]
    for s in range(n_sub):
        current, pending = pending, []
        todo = column_blocks(s + 1) if s + 1 < n_sub else []

        def advance(todo=todo, done=pending):
            if todo:
                done.append(todo.pop(0)())

        tile_fn(s, advance, *current)
        while todo:
            advance()


def _tile_spec(tile_id, t):
    return pl.BlockSpec((None, None, t, TILE_W), lambda b, i: (tile_id, b, i, 0))


def _layer_spec(layer, rows, width):
    return pl.BlockSpec((None, rows, width), lambda b, i: (layer, 0, 0))


def _block_inverse_all(n0s, ri, ci, c):
    eye = jnp.where(ri == ci, 1.0, 0.0)
    lvl1 = (ri >> 1) == (ci >> 1)
    xs = [eye + jnp.where(lvl1, n0, 0.0) for n0 in n0s]
    for lv in range(2, int(np.log2(c)) + 1):
        m = ((ri >> lv) == (ci >> lv)) & ((ri >> (lv - 1)) != (ci >> (lv - 1)))
        xbs = [x.astype(BF16) for x in xs]
        xns = [_dot(xb, jnp.where(m, n0, 0.0).astype(BF16)).astype(BF16) for xb, n0 in zip(xbs, n0s)]
        xs = [x + _dot(xn, xb) for x, xn, xb in zip(xs, xns, xbs)]
    return xs


def _rwkv_kernel(u_ref, w_ref, mu_ref, vec_ref, up_ref, o_ref,
                 carry_ref, st_ref, *, t_blk, n_sub, chunk, pair_t):
    @pl.when(pl.program_id(1) == 0)
    def _():
        carry_ref[...] = jnp.zeros_like(carry_ref)
        st_ref[...] = jnp.zeros_like(st_ref)

    def tile(s, advance, r, k, v, z, codes):
        _rwkv_tile(r, k, v, codes, z, advance, mu_ref, vec_ref, up_ref, _sub(o_ref, s, t_blk), carry_ref, st_ref,
                   t_blk=t_blk, chunk=chunk, pair_t=pair_t)

    _run_fused_tiles(tile, (TILE_W, TILE_W, TILE_W, TILE_W, SMALL_W), u_ref, w_ref, n_sub, t_blk)


def _rwkv_tile(r_ref, k_ref, v_ref, sm_ref, z_ref, advance, mu_ref, vec_ref, up_ref, o_ref,
               carry_ref, st_ref, *, t_blk, chunk, pair_t):
    T, C, TP = t_blk, chunk, pair_t
    n_chunk = T // C
    n_part = T // TP
    c_per_part = TP // C
    for _ in range(5):
        advance()

    row0 = _iota2((T, 1), 0) == 0

    def shift_mix(x, slot, width, mu):
        prev = carry_ref[slot:slot + 1, 0:width]
        xp = jnp.where(row0, prev, pltpu.roll(x, 1, 0))
        carry_ref[slot:slot + 1, 0:width] = x[T - 1:T, :]
        return x + (xp - x) * mu

    r = shift_mix(r_ref[...], 0, TILE_W, mu_ref[0:1, :])
    k = shift_mix(k_ref[...], 1, TILE_W, mu_ref[1:2, :])
    v = shift_mix(v_ref[...], 2, TILE_W, mu_ref[2:3, :])
    codes = shift_mix(sm_ref[:, 0:LANES], 3, LANES, mu_ref[3:4, 0:LANES])

    w0, a0 = vec_ref[0:1, :], vec_ref[1:2, :]
    k_k, k_a, r_k = vec_ref[2:3, :], vec_ref[3:4, :], vec_ref[4:5, :]
    ln_g, ln_b = vec_ref[5:6, :], vec_ref[6:7, :]

    lane = _iota2((T, LANES), 1)
    codes_t = jnp.where(lane < RW_RANK, jnp.tanh(codes), codes).astype(BF16)
    proj = _dot(codes_t, up_ref[...])
    w = -_softplus(-(w0 + proj[:, 0:TILE_W])) - 0.5
    lw = -jnp.exp(w)
    a = _sigmoid(a0 + proj[:, TILE_W:2 * TILE_W])

    ones64 = _seg_ones(TILE_W, RW_N)
    kkr = k * k_k
    ss = _dot((kkr * kkr).astype(BF16), ones64)
    kk = kkr / jnp.maximum(jnp.sqrt(ss), 1e-12)
    k2 = k * (1.0 + (a - 1.0) * k_a)
    alpha = -kk
    beta = kk * a

    ri_t, ci_t, same_t = _chunk_masks(T, C)
    tri = jnp.where(same_t & (ci_t <= ri_t), 1.0, 0.0).astype(BF16)
    blk = jnp.where(same_t, 1.0, 0.0).astype(BF16)
    cw = _mask_dot_left(tri, lw)
    cwl = _mask_dot_left(blk, lw)
    e_in = jnp.exp(cw)
    e_out = jnp.exp(-cw)
    e_end = jnp.exp(cwl - cw)
    a_t = alpha * jnp.exp(cw - lw)
    r_t = r * e_in
    b_t = beta * e_out
    k_t = k2 * e_out
    b_h = (beta * e_end).astype(BF16)
    k_h = (k2 * e_end).astype(BF16)
    p_tot = jnp.exp(cwl)

    a_tt = a_t.T.astype(BF16)
    r_tt = r_t.T.astype(BF16)
    v_tt = v.T.astype(BF16)

    ri, ci, same = _chunk_masks(TP, C)
    up_strict = same & (ci > ri)
    up_incl = same & (ci >= ri)
    lane_head = _iota2((1, LANES), 1) >> 6
    heads = range(RW_H)
    units = [(p, h) for p in range(n_part) for h in heads]

    def hrows(h):
        return slice(h * RW_N, (h + 1) * RW_N)

    m_ba, m_br, m_k = {}, {}, {}
    for u in units:
        p, h = u
        tok = slice(p * TP, (p + 1) * TP)
        win = slice((h // 2) * LANES, (h // 2 + 1) * LANES)
        hmask = lane_head == (h % 2)
        lhs = jnp.concatenate([jnp.where(hmask, b_t[tok, win], 0.0),
                               jnp.where(hmask, k_t[tok, win], 0.0)], axis=0).astype(BF16)
        rhs = jnp.concatenate([a_tt[win, tok], r_tt[win, tok]], axis=1)
        m = _dot(lhs, rhs)
        m_ba[u] = jnp.where(up_strict, m[0:TP, 0:TP], 0.0)
        m_br[u] = jnp.where(up_incl, m[0:TP, TP:2 * TP], 0.0).astype(BF16)
        m_k[u] = jnp.concatenate([jnp.where(up_strict, m[TP:2 * TP, 0:TP], 0.0),
                                  jnp.where(up_incl, m[TP:2 * TP, TP:2 * TP], 0.0)], axis=1).astype(BF16)
    inv = _block_inverse_all([m_ba[u] for u in units], ri, ci, C)
    tts = {u: x.astype(BF16) for u, x in zip(units, inv)}
    g2 = {(p, h): _dot(v_tt[hrows(h), p * TP:(p + 1) * TP], m_k[(p, h)]) for p, h in units}
    wu = {(p, h): _dot(jnp.concatenate([a_tt[hrows(h), p * TP:(p + 1) * TP],
                                        g2[(p, h)][:, 0:TP].astype(BF16)], axis=0), tts[(p, h)])
          for p, h in units}
    w_t = {u: wu[u][0:RW_N, :].astype(BF16) for u in units}
    kv = [[_dot(v_tt[hrows(h), c * C:(c + 1) * C], k_h[c * C:(c + 1) * C, hrows(h)])
           for c in range(n_chunk)] for h in heads]

    st = [st_ref[hrows(h), :] for h in heads]
    yts = [[] for _ in heads]
    for c in range(n_chunk):
        c0 = c * C
        p = c // c_per_part
        cl = (c % c_per_part) * C
        d1 = [_dot(st[h].astype(BF16),
                   jnp.concatenate([w_t[(p, h)][:, cl:cl + C], r_tt[hrows(h), c0:c0 + C]], axis=1))
              for h in heads]
        ut = [(d1[h][:, 0:C] + wu[(p, h)][RW_N:2 * RW_N, cl:cl + C]).astype(BF16) for h in heads]
        d2 = [_dot(ut[h], jnp.concatenate([m_br[(p, h)][cl:cl + C, cl:cl + C],
                                            b_h[c0:c0 + C, hrows(h)]], axis=1))
              for h in heads]
        for h in heads:
            yts[h].append(d1[h][:, C:2 * C] + g2[(p, h)][:, TP + cl:TP + cl + C] + d2[h][:, 0:C])
            st[h] = st[h] * p_tot[c0:c0 + 1, hrows(h)] + d2[h][:, C:C + RW_N] + kv[h][c]
    yt_heads = []
    for h in heads:
        st_ref[hrows(h), :] = st[h]
        yt = jnp.concatenate(yts[h], axis=1)
        d = yt - jnp.mean(yt, axis=0, keepdims=True)
        yt_heads.append(d * lax.rsqrt(jnp.mean(d * d, axis=0, keepdims=True) + RW_GN_EPS))
    y = jnp.concatenate(yt_heads, axis=0).T * ln_g + ln_b
    bonus = _dot((r * k2 * r_k).astype(BF16), ones64) * v
    o_ref[...] = ((y + bonus) * _silu(z_ref[...])).astype(o_ref.dtype)


def _rwkv_call(u, w, layer, mu, vec, up):
    bsz, seq, d = u.shape
    t, n_sub = _mix_tiling(seq, RW_SUB)
    kern = functools.partial(_rwkv_kernel, t_blk=t // n_sub, n_sub=n_sub, chunk=min(RW_CHUNK, t),
                             pair_t=min(RW_PAIR_T, t))
    return pl.pallas_call(
        kern,
        grid=(bsz, seq // t),
        in_specs=[_u_spec(t, d), _layer_spec(layer, d, w.shape[2]),
                  _layer_spec(layer, 8, TILE_W), _layer_spec(layer, 8, TILE_W),
                  _layer_spec(layer, LANES, 2 * TILE_W)],
        out_specs=pl.BlockSpec((None, t, BRANCH_W), lambda b, i: (b, i, 0)),
        out_shape=jax.ShapeDtypeStruct((bsz, seq, BRANCH_W), BF16),
        scratch_shapes=[pltpu.VMEM((8, TILE_W), F32), pltpu.VMEM((RW_H * RW_N, RW_N), F32)],
        compiler_params=_cparams(("parallel", "arbitrary")),
        name="rwkv7_mix",
    )(u, w, mu, vec, up)


def _gla_recurrence(q, k, v, logg, advance, st_ref, qs_ref, ks_ref, vs_ref, gs_ref, os_ref, *,
                    n_head, dk, t_blk, chunk):
    T, C = t_blk, chunk
    n_chunk = T // C
    dv = LANES

    def window(h):
        if dk == LANES:
            return h * LANES, None
        return (h // 2) * LANES, (_iota2((1, LANES), 1) >> 6) == (h % 2)

    ri, ci, same = _chunk_masks(T, C)
    tri = jnp.where(same & (ci <= ri), 1.0, 0.0).astype(BF16)
    blk = jnp.where(same, 1.0, 0.0).astype(BF16)
    b = _mask_dot_left(tri, logg)
    btot = _mask_dot_left(blk, logg)
    worst = jnp.max(jnp.max(-btot, axis=1, keepdims=True), axis=0, keepdims=True)
    heads = range(n_head)
    chunks = range(n_chunk)
    st_in = [st_ref[h * dv:(h + 1) * dv, :] for h in heads]

    advance()
    q_t = (q * jnp.exp(b)).astype(BF16)
    k_t = k * jnp.exp(-b)
    k_h = k * jnp.exp(btot - b)
    d_tot = jnp.exp(btot)
    v_tt = v.T.astype(BF16)
    vb = v.astype(BF16)
    causal = same & (ci <= ri)
    qw, ktw, khw = [], [], []
    for h in heads:
        w0, hm = window(h)
        qw.append(q_t[:, w0:w0 + LANES])
        kt_h, kh_h = k_t[:, w0:w0 + LANES], k_h[:, w0:w0 + LANES]
        if hm is not None:
            kt_h, kh_h = jnp.where(hm, kt_h, 0.0), jnp.where(hm, kh_h, 0.0)
        ktw.append(kt_h.astype(BF16))
        khw.append(kh_h.astype(BF16))
    advance()
    att = [jnp.where(causal, _dot_nt(qw[h], ktw[h]), 0.0).astype(BF16) for h in heads]
    advance()
    ds = [[_dot(v_tt[h * dv:(h + 1) * dv, c * C:(c + 1) * C], khw[h][c * C:(c + 1) * C, :]) for c in chunks]
          for h in heads]
    advance()
    sts = []
    for h in heads:
        w0, _ = window(h)
        st = st_in[h]
        row = []
        for c in chunks:
            row.append(st.astype(BF16))
            st = st * d_tot[c * C:c * C + 1, w0:w0 + LANES] + ds[h][c]
        st_ref[h * dv:(h + 1) * dv, :] = st
        sts.append(row)
    for h in heads:
        inter = [_dot_nt(qw[h][c * C:(c + 1) * C, :], sts[h][c]) for c in chunks]
        os_ref[:, h * dv:(h + 1) * dv] = (_dot(att[h], vb[:, h * dv:(h + 1) * dv])
                                          + jnp.concatenate(inter, axis=0))

    @pl.when(worst[0, 0] > SAFE_EXP)
    def _():
        qs_ref[...] = q
        ks_ref[...] = k
        gs_ref[...] = logg
        vs_ref[...] = v
        eye = _iota2((LANES, LANES), 0) == _iota2((LANES, LANES), 1)
        for h in range(n_head):
            w0, hm = window(h)

            def body(t8, st, w0=w0, hm=hm, h=h):
                rows = pl.ds(pl.multiple_of(t8 * 8, 8), 8)
                q8 = qs_ref[rows, w0:w0 + LANES]
                k8 = ks_ref[rows, w0:w0 + LANES]
                e8 = jnp.exp(gs_ref[rows, w0:w0 + LANES])
                v8 = vs_ref[rows, h * dv:(h + 1) * dv]
                if hm is not None:
                    k8 = jnp.where(hm, k8, 0.0)
                    q8 = jnp.where(hm, q8, 0.0)
                o_rows = []
                for j in range(8):
                    v_col = jnp.sum(jnp.where(eye, v8[j:j + 1, :], 0.0), axis=1, keepdims=True)
                    st = st * e8[j:j + 1, :] + v_col * k8[j:j + 1, :]
                    o_col = jnp.sum(st * q8[j:j + 1, :], axis=1, keepdims=True)
                    o_rows.append(jnp.sum(jnp.where(eye, o_col, 0.0), axis=0, keepdims=True))
                os_ref[rows, h * dv:(h + 1) * dv] = jnp.concatenate(o_rows, axis=0)
                return st

            st_ref[h * dv:(h + 1) * dv, :] = lax.fori_loop(0, T // 8, body, st_in[h])


def _head_rms_gate(o, g, z, n_head):
    outs = []
    for h in range(n_head):
        oh = o[:, h * LANES:(h + 1) * LANES]
        outs.append(oh * lax.rsqrt(jnp.mean(oh * oh, axis=-1, keepdims=True) + NORM_EPS))
    return jnp.concatenate(outs, axis=1) * g * _silu(z)


def _gla_kernel(u_ref, w_ref, up_ref, vec_ref, o_ref, st_ref, *scratch, t_blk, n_sub, chunk):
    @pl.when(pl.program_id(1) == 0)
    def _():
        st_ref[...] = jnp.zeros_like(st_ref)

    def tile(s, advance, qk, v, z, code):
        _gla_tile(qk, v, code, z, advance, up_ref, vec_ref, _sub(o_ref, s, t_blk), st_ref, *scratch,
                  t_blk=t_blk, chunk=chunk)

    _run_fused_tiles(tile, (TILE_W, TILE_W, TILE_W, SMALL_W), u_ref, w_ref, n_sub, t_blk)


def _gla_tile(qk, v, code, z, advance, up_ref, vec_ref, o_ref,
              st_ref, qs_ref, ks_ref, vs_ref, gs_ref, os_ref, *, t_blk, chunk):
    hk = GLA_H * GLA_DK
    q = qk[:, 0:hk] * (GLA_DK ** -0.5)
    k = qk[:, hk:2 * hk]
    logits = _dot(code.astype(BF16), up_ref[...]) + vec_ref[0:1, 0:hk]
    logg = _log_sigmoid(logits) / GLA_LOGIT_NORM
    _gla_recurrence(q, k, v, logg, advance, st_ref, qs_ref, ks_ref, vs_ref, gs_ref, os_ref,
                    n_head=GLA_H, dk=GLA_DK, t_blk=t_blk, chunk=chunk)
    o_ref[...] = _head_rms_gate(os_ref[...], vec_ref[1:2, :], z, GLA_H).astype(o_ref.dtype)


def _hgrn_kernel(u_ref, w_ref, vec_ref, o_ref, st_ref, *scratch, t_blk, n_sub, chunk):
    @pl.when(pl.program_id(1) == 0)
    def _():
        st_ref[...] = jnp.zeros_like(st_ref)

    def tile(s, advance, q, f, i, z):
        _hgrn_tile(q, f, i, z, advance, vec_ref, _sub(o_ref, s, t_blk), st_ref, *scratch,
                   t_blk=t_blk, chunk=chunk)

    _run_fused_tiles(tile, (TILE_W,) * 4, u_ref, w_ref, n_sub, t_blk)


def _hgrn_tile(q, f, i, z, advance, vec_ref, o_ref,
               st_ref, qs_ref, ks_ref, vs_ref, gs_ref, os_ref, *, t_blk, chunk):
    lb = vec_ref[0:1, :]
    g = lb + (1.0 - lb) * _sigmoid(f)
    _gla_recurrence(_silu(q), 1.0 - g, i, jnp.log(g), advance, st_ref,
                    qs_ref, ks_ref, vs_ref, gs_ref, os_ref,
                    n_head=HG_H, dk=LANES, t_blk=t_blk, chunk=chunk)
    o_ref[...] = _head_rms_gate(os_ref[...], vec_ref[1:2, :], z, HG_H).astype(o_ref.dtype)


def _gla_scratch(t, key_w):
    return [pltpu.VMEM((4 * LANES, LANES), F32), pltpu.VMEM((t, key_w), F32), pltpu.VMEM((t, key_w), F32),
            pltpu.VMEM((t, BRANCH_W), F32), pltpu.VMEM((t, key_w), F32), pltpu.VMEM((t, BRANCH_W), F32)]


def _gla_call(u, w, layer, up, vec):
    bsz, seq, d = u.shape
    t, n_sub = _mix_tiling(seq, FUSED_SUB)
    kern = functools.partial(_gla_kernel, t_blk=t // n_sub, n_sub=n_sub, chunk=min(GLA_CHUNK, t))
    return pl.pallas_call(
        kern,
        grid=(bsz, seq // t),
        in_specs=[_u_spec(t, d), _layer_spec(layer, d, w.shape[2]),
                  _layer_spec(layer, LANES, GLA_H * GLA_DK), _layer_spec(layer, 8, TILE_W)],
        out_specs=pl.BlockSpec((None, t, BRANCH_W), lambda b, i: (b, i, 0)),
        out_shape=jax.ShapeDtypeStruct((bsz, seq, BRANCH_W), BF16),
        scratch_shapes=_gla_scratch(t // n_sub, GLA_H * GLA_DK),
        compiler_params=_cparams(("parallel", "arbitrary")),
        name="gla_mix",
    )(u, w, up, vec)


def _hgrn_call(u, w, layer, vec):
    bsz, seq, d = u.shape
    t, n_sub = _mix_tiling(seq, FUSED_SUB)
    chunk = HG_CHUNK_FIRST if layer == 0 else HG_CHUNK
    kern = functools.partial(_hgrn_kernel, t_blk=t // n_sub, n_sub=n_sub, chunk=min(chunk, t))
    return pl.pallas_call(
        kern,
        grid=(bsz, seq // t),
        in_specs=[_u_spec(t, d), _layer_spec(layer, d, w.shape[2]), _layer_spec(layer, 8, TILE_W)],
        out_specs=pl.BlockSpec((None, t, BRANCH_W), lambda b, i: (b, i, 0)),
        out_shape=jax.ShapeDtypeStruct((bsz, seq, BRANCH_W), BF16),
        scratch_shapes=_gla_scratch(t // n_sub, TILE_W),
        compiler_params=_cparams(("parallel", "arbitrary")),
        name="hgrn2_mix",
    )(u, w, vec)


def _mlstm_kernel(u_ref, w_ref, conv_ref, vec_ref, o_ref,
                  prev_ref, ct_ref, n_ref, m_ref, *, t_blk, n_sub, chunk):
    @pl.when(pl.program_id(1) == 0)
    def _():
        prev_ref[...] = jnp.zeros_like(prev_ref)
        ct_ref[...] = jnp.zeros_like(ct_ref)
        n_ref[...] = jnp.zeros_like(n_ref)
        m_ref[...] = jnp.full_like(m_ref, STAB_INIT)

    def tile(s, advance, qk, v, z, gates):
        _mlstm_tile(qk, v, gates, z, advance, conv_ref, vec_ref, _sub(o_ref, s, t_blk), prev_ref, ct_ref, n_ref,
                    m_ref, t_blk=t_blk, chunk=chunk)

    _run_fused_tiles(tile, (TILE_W, TILE_W, TILE_W, SMALL_W), u_ref, w_ref, n_sub, t_blk)


def _mlstm_tile(x, v, gates, z, advance, conv_ref, vec_ref, o_ref,
                prev_ref, ct_ref, n_ref, m_ref, *, t_blk, chunk):
    T, C = t_blk, chunk
    n_chunk = T // C
    heads = range(ML_H)
    for _ in range(4):
        advance()

    xx = jnp.concatenate([prev_ref[...], x], axis=0)
    prev_ref[...] = x[T - 8:T, :]
    conv = x * conv_ref[ML_CONV - 1:ML_CONV, :]
    for d in range(1, ML_CONV):
        conv = conv + xx[8 - d:8 - d + T, :] * conv_ref[ML_CONV - 1 - d:ML_CONV - d, :]
    qk = _silu(conv)
    hq = ML_H * ML_DQK
    qb = qk[:, 0:hq].astype(BF16)
    kf = qk[:, hq:2 * hq] * (ML_DQK ** -0.5)
    kb = kf.astype(BF16)
    vb = v.astype(BF16)
    v_tt = v.T.astype(BF16)

    i_col = gates + vec_ref[0:1, 0:LANES]
    lf_col = _log_sigmoid(gates + vec_ref[1:2, 0:LANES])
    ri_t, ci_t, same_t = _chunk_masks(T, C)
    tri = jnp.where(same_t & (ci_t <= ri_t), 1.0, 0.0).astype(BF16)
    cf_col = _mask_dot_left(tri, lf_col)
    cf_row = cf_col.T
    i_row = i_col.T
    li, lf = ML_I_LANE, ML_F_LANE
    causal = _iota2((C, C), 1) <= _iota2((C, C), 0)
    lane_head = _iota2((1, LANES), 1) >> 6

    ct = [ct_ref[h * LANES:(h + 1) * LANES, :] for h in heads]
    n_row = [n_ref[h:h + 1, :] for h in heads]
    m_prev = [m_ref[h:h + 1, 0:1] for h in heads]
    outs = [[] for _ in heads]
    for c in range(n_chunk):
        tok = slice(c * C, (c + 1) * C)
        for h in heads:
            win = slice((h // 2) * LANES, (h // 2 + 1) * LANES)
            hm = lane_head == (h % 2)
            cf_i = cf_col[tok, lf + h:lf + h + 1]
            i_i = i_col[tok, li + h:li + h + 1]
            cf_j = cf_row[lf + h:lf + h + 1, tok]
            i_j = i_row[li + h:li + h + 1, tok]
            log_d = jnp.where(causal, cf_i - cf_j + i_j, -jnp.inf)
            log_inter = cf_i + m_prev[h]
            m_t = jnp.maximum(log_inter, jnp.max(log_d, axis=-1, keepdims=True))
            qh = jnp.where(hm, qb[tok, win], 0.0)
            s = _dot_nt(qh, kb[tok, win]) * jnp.exp(log_d - m_t)
            w_inter = jnp.exp(log_inter - m_t)
            num = (_dot(s.astype(BF16), vb[tok, h * LANES:(h + 1) * LANES])
                   + w_inter * _dot_nt(qh, ct[h].astype(BF16)))
            den = jnp.sum(s, axis=-1, keepdims=True) + w_inter * jnp.sum(
                jnp.where(hm, qk[tok, win], 0.0) * n_row[h], axis=-1, keepdims=True)
            outs[h].append(num / jnp.maximum(jnp.abs(den), jnp.exp(-m_t)))
            m_new = m_t[C - 1:C, :]
            cf_last = cf_i[C - 1:C, :]
            w_carry = jnp.exp(cf_last + m_prev[h] - m_new)
            kw = jnp.where(hm, kf[tok, win], 0.0) * jnp.exp(cf_last - cf_i + i_i - m_new)
            ct[h] = w_carry * ct[h] + _dot(v_tt[h * LANES:(h + 1) * LANES, tok], kw.astype(BF16))
            n_row[h] = w_carry * n_row[h] + jnp.sum(kw, axis=0, keepdims=True)
            m_prev[h] = m_new

    ln = []
    for h in heads:
        ct_ref[h * LANES:(h + 1) * LANES, :] = ct[h]
        n_ref[h:h + 1, :] = n_row[h]
        m_ref[h:h + 1, :] = jnp.broadcast_to(m_prev[h], (1, LANES))
        oh = jnp.concatenate(outs[h], axis=0)
        d = oh - jnp.mean(oh, axis=-1, keepdims=True)
        ln.append(d * lax.rsqrt(jnp.mean(d * d, axis=-1, keepdims=True) + NORM_EPS))
    o_ref[...] = (jnp.concatenate(ln, axis=1) * vec_ref[2:3, :] * _silu(z)).astype(o_ref.dtype)


def _mlstm_call(u, w, layer, conv_w, vec):
    bsz, seq, d = u.shape
    t, n_sub = _mix_tiling(seq, FUSED_SUB)
    kern = functools.partial(_mlstm_kernel, t_blk=t // n_sub, n_sub=n_sub, chunk=min(ML_CHUNK, t))
    return pl.pallas_call(
        kern,
        grid=(bsz, seq // t),
        in_specs=[_u_spec(t, d), _layer_spec(layer, d, w.shape[2]),
                  _layer_spec(layer, 8, TILE_W), _layer_spec(layer, 8, TILE_W)],
        out_specs=pl.BlockSpec((None, t, BRANCH_W), lambda b, i: (b, i, 0)),
        out_shape=jax.ShapeDtypeStruct((bsz, seq, BRANCH_W), BF16),
        scratch_shapes=[pltpu.VMEM((8, TILE_W), F32), pltpu.VMEM((ML_H * LANES, LANES), F32),
                        pltpu.VMEM((8, LANES), F32), pltpu.VMEM((8, LANES), F32)],
        compiler_params=_cparams(("parallel", "arbitrary")),
        name="mlstm_mix",
    )(u, w, conv_w, vec)


def _merge_kernel(x_ref, ya_ref, yb_ref, yc_ref, yd_ref, *rest, final):
    gate_refs = rest[0:N_GATE_TILES]
    wb_ref, wo_ref, res_gate_ref, ng_ref, nsc_ref, nsh_ref = rest[N_GATE_TILES:N_GATE_TILES + 6]
    out_refs = rest[N_GATE_TILES + 6:]
    per_branch = N_GATE_TILES // N_BRANCH
    merged = None
    for m, y_ref in enumerate((ya_ref, yb_ref, yc_ref, yd_ref)):
        br = _dot(y_ref[...], wb_ref[m])
        gates = jnp.concatenate([gate_refs[per_branch * m + j][...] for j in range(per_branch)], axis=1)
        term = gates * br
        merged = term if merged is None else merged + term
    x_new = x_ref[...] + res_gate_ref[...] * _dot(merged.astype(BF16), wo_ref[...])
    ms = jnp.mean(x_new * x_new, axis=-1, keepdims=True)
    normed = x_new * lax.rsqrt(ms + NORM_EPS) * ng_ref[...]
    if final:
        out_refs[0][...] = normed
    else:
        out_refs[0][...] = x_new
        out_refs[1][...] = (normed * (1.0 + nsc_ref[...]) + nsh_ref[...]).astype(BF16)


def _merge_call(x, ys, gates, layer, w_branch, w_out, res_gate, next_g, next_scale, next_shift, final):
    bsz, seq, d = x.shape
    tm = min(seq, 512)
    row = pl.BlockSpec((None, tm, d), lambda b, i: (b, i, 0))
    y_spec = pl.BlockSpec((None, tm, BRANCH_W), lambda b, i: (b, i, 0))
    vec = pl.BlockSpec((None, 1, d), lambda b, i: (b, 0, 0))
    gate_specs = [_tile_spec(j, tm) for j in range(N_GATE_TILES)]
    out_shape = [jax.ShapeDtypeStruct((bsz, seq, d), F32)]
    out_specs = [row]
    if not final:
        out_shape.append(jax.ShapeDtypeStruct((bsz, seq, d), BF16))
        out_specs.append(row)
    return pl.pallas_call(
        functools.partial(_merge_kernel, final=final),
        grid=(bsz, seq // tm),
        in_specs=[row, y_spec, y_spec, y_spec, y_spec] + gate_specs
        + [pl.BlockSpec((None, N_BRANCH, BRANCH_W, d), lambda b, i: (layer, 0, 0, 0)),
           pl.BlockSpec((None, d, d), lambda b, i: (layer, 0, 0)),
           vec, pl.BlockSpec((1, d), lambda b, i: (0, 0)), vec, vec],
        out_specs=out_specs,
        out_shape=out_shape,
        compiler_params=_cparams(("parallel", "parallel")),
        name="merge_out",
    )(x, *ys, *([gates] * N_GATE_TILES), w_branch, w_out, res_gate.reshape(bsz, 1, d),
      next_g.reshape(1, d), next_scale.reshape(bsz, 1, d), next_shift.reshape(bsz, 1, d))


def _pad_last(a, width):
    return jnp.pad(a, [(0, 0)] * (a.ndim - 1) + [(0, width - a.shape[-1])])


def _rows(rows, width, n_rows=8):
    out = [_pad_last(r.astype(F32), width)[:, None, :] for r in rows]
    out.append(jnp.zeros((rows[0].shape[0], n_rows - len(rows), width), F32))
    return jnp.concatenate(out, axis=1)


def _relayout_w_in(w):
    w = w.astype(BF16)
    o = 0

    def take(n):
        nonlocal o
        s = w[:, :, o:o + n]
        o += n
        return s

    rw_r, rw_k, rw_v, rw_wc, rw_ac, rw_z = take(512), take(512), take(512), take(64), take(64), take(512)
    gq, gk, gv, gcode, gz = take(256), take(256), take(512), take(16), take(512)
    mqk, mv, mi, mf, mz = take(512), take(512), take(4), take(4), take(512)
    hq, hf, hi, hz = take(512), take(512), take(512), take(512)
    merge = take(N_BRANCH * D_MODEL)
    lead = jnp.zeros(w.shape[:2] + (ML_I_LANE,), w.dtype)
    blocks = {
        "rw": [rw_r, rw_k, rw_v, rw_z, rw_wc, rw_ac],
        "gla": [gq, gk, gv, gz, _pad_last(gcode, SMALL_W)],
        "ml": [mqk, mv, mz, _pad_last(jnp.concatenate([lead, mi, mf], axis=2), SMALL_W)],
        "hg": [hq, hf, hi, hz],
        "gate": [merge],
    }
    return {name: jnp.concatenate(cols, axis=2) for name, cols in blocks.items()}


def kernel(x, c, norm_g, ada_w, ada_b, w_in, rw_mu, rw_w0, rw_w_up, rw_a0, rw_a_up, rw_k_k, rw_k_a,
           rw_r_k, rw_ln_g, rw_ln_b, gla_gk_up, gla_gk_b, gla_norm_g, ml_conv_w, ml_i_b, ml_f_b,
           ml_norm_g, hg_lb_logits, hg_norm_g, w_branch, w_out, final_g):
    depth = w_in.shape[0]
    d = x.shape[-1]
    lb_p = jax.nn.softmax(hg_lb_logits.astype(F32), axis=0)
    lower_bounds = jnp.cumsum(lb_p, axis=0) - lb_p[0]

    w_cols = _relayout_w_in(w_in)
    rw_mu_rows = _rows([rw_mu[:, 0:512], rw_mu[:, 512:1024], rw_mu[:, 1024:1536], rw_mu[:, 1536:1664]], TILE_W)
    rw_vec = _rows([rw_w0, rw_a0, rw_k_k, rw_k_a, rw_r_k, rw_ln_g, rw_ln_b], TILE_W)
    zeros = jnp.zeros((depth, RW_RANK, BRANCH_W), F32)
    rw_up = jnp.concatenate([jnp.concatenate([rw_w_up, zeros], axis=2),
                             jnp.concatenate([zeros, rw_a_up], axis=2)], axis=1).astype(BF16)
    gla_up = jnp.pad(gla_gk_up, ((0, 0), (0, LANES - GLA_RANK), (0, 0))).astype(BF16)
    gla_vec = _rows([gla_gk_b, gla_norm_g], TILE_W)
    ml_conv = jnp.pad(ml_conv_w.astype(F32), ((0, 0), (0, 8 - ML_CONV), (0, 0)))
    ml_vec = _rows([jnp.pad(ml_i_b, ((0, 0), (ML_I_LANE, 0))),
                    jnp.pad(ml_f_b, ((0, 0), (ML_F_LANE, 0))), ml_norm_g], TILE_W)
    hg_vec = _rows([lower_bounds, hg_norm_g], TILE_W)
    wb = w_branch.astype(BF16)
    wo = w_out.astype(BF16)

    mod = _ada_call(c, ada_w, ada_b)
    u = _norm_call(x, norm_g[0], mod[0, :, d:2 * d], mod[0, :, 0:d])
    for l in range(depth):
        gates = _proj_call(u, w_cols["gate"], l, group=4, rows=2048, gate=True, name="gate_proj")
        y_a = _rwkv_call(u, w_cols["rw"], l, rw_mu_rows, rw_vec, rw_up)
        y_b = _gla_call(u, w_cols["gla"], l, gla_up, gla_vec)
        y_c = _mlstm_call(u, w_cols["ml"], l, ml_conv, ml_vec)
        y_d = _hgrn_call(u, w_cols["hg"], l, hg_vec)
        final = l == depth - 1
        if final:
            next_g, next_scale, next_shift = final_g, mod[l, :, d:2 * d], mod[l, :, 0:d]
        else:
            next_g, next_scale, next_shift = norm_g[l + 1], mod[l + 1, :, d:2 * d], mod[l + 1, :, 0:d]
        res = _merge_call(x, (y_a, y_b, y_c, y_d), gates, l, wb, wo, mod[l, :, 2 * d:3 * d],
                          next_g, next_scale, next_shift, final)
        if final:
            x = res[0]
        else:
            x, u = res
    return x
```

```python
import functools

import jax
import jax.numpy as jnp
import numpy as np
from jax import lax
from jax.experimental import pallas as pl
from jax.experimental.pallas import tpu as pltpu

F32 = jnp.float32
BF16 = jnp.bfloat16

D_MODEL = 1024
N_BRANCH = 4
BRANCH_W = 512
RW_N = 64
RW_H = 8
RW_RANK = 64
RW_GN_EPS = 64e-5
GLA_H = 4
GLA_DK = 64
GLA_RANK = 16
GLA_LOGIT_NORM = 16.0
ML_H = 4
ML_DQK = 64
ML_CONV = 4
STAB_INIT = -1e30
HG_H = 4
NORM_EPS = 1e-6

LANES = 128
TILE_W = 512
MIX_T = 256
RW_SUB = 2
FUSED_SUB = 4
SMALL_W = 128
RW_CHUNK = 64
RW_PAIR_T = 128
GLA_CHUNK = 128
HG_CHUNK_FIRST = 16
HG_CHUNK = 32
ML_CHUNK = 256
SAFE_EXP = 80.0
VMEM_LIMIT = 56 * 1024 * 1024

N_GATE_TILES = N_BRANCH * D_MODEL // TILE_W
ML_I_LANE = 16
ML_F_LANE = 20


def _cparams(sem):
    return pltpu.CompilerParams(dimension_semantics=sem, vmem_limit_bytes=VMEM_LIMIT)


def _sigmoid(x):
    return 1.0 / (1.0 + jnp.exp(-x))


def _silu(x):
    return x * _sigmoid(x)


def _softplus(x):
    return jnp.maximum(x, 0.0) + jnp.log(1.0 + jnp.exp(-jnp.abs(x)))


def _log_sigmoid(x):
    return -_softplus(-x)


def _split2(x):
    hi = x.astype(BF16)
    lo = (x - hi.astype(F32)).astype(BF16)
    return hi, lo


def _dot(a, b):
    return jnp.dot(a, b, preferred_element_type=F32)


def _dot_nt(a, b):
    return lax.dot_general(a, b, (((1,), (1,)), ((), ())), preferred_element_type=F32)


def _mask_dot_left(m01, x):
    hi, lo = _split2(x)
    return _dot(m01, hi) + _dot(m01, lo)


def _iota2(shape, dim):
    return lax.broadcasted_iota(jnp.int32, shape, dim)


def _chunk_masks(t, c):
    sh = int(np.log2(c))
    ri = _iota2((t, t), 0)
    ci = _iota2((t, t), 1)
    same = (ri >> sh) == (ci >> sh)
    return ri, ci, same


def _seg_ones(width, seg):
    sh = int(np.log2(seg))
    ri = _iota2((width, width), 0)
    ci = _iota2((width, width), 1)
    return jnp.where((ri >> sh) == (ci >> sh), 1.0, 0.0).astype(BF16)


def _rms_mod(x, g, scale, shift):
    ms = jnp.mean(x * x, axis=-1, keepdims=True)
    return x * lax.rsqrt(ms + NORM_EPS) * g * (1.0 + scale) + shift


def _ada_kernel(c_ref, w_ref, b_ref, o_ref):
    cond = _silu(c_ref[...]).astype(BF16)
    o_ref[...] = _dot(cond, w_ref[...].astype(BF16)) + b_ref[...]


def _ada_call(c, ada_w, ada_b):
    depth, d, n3 = ada_w.shape
    bsz = c.shape[0]
    tn = 1024
    return pl.pallas_call(
        _ada_kernel,
        grid=(depth, n3 // tn),
        in_specs=[
            pl.BlockSpec((bsz, d), lambda l, j: (0, 0)),
            pl.BlockSpec((None, d, tn), lambda l, j: (l, 0, j)),
            pl.BlockSpec((None, 1, tn), lambda l, j: (l, 0, j)),
        ],
        out_specs=pl.BlockSpec((None, bsz, tn), lambda l, j: (l, 0, j)),
        out_shape=jax.ShapeDtypeStruct((depth, bsz, n3), F32),
        compiler_params=_cparams(("parallel", "parallel")),
        name="ada_mod",
    )(c, ada_w, ada_b.reshape(depth, 1, n3))


def _norm_kernel(x_ref, g_ref, sc_ref, sh_ref, u_ref):
    u_ref[...] = _rms_mod(x_ref[...], g_ref[...], sc_ref[...], sh_ref[...]).astype(u_ref.dtype)


def _norm_call(x, g, scale, shift):
    bsz, seq, d = x.shape
    tm = min(seq, 1024)
    return pl.pallas_call(
        _norm_kernel,
        grid=(bsz, seq // tm),
        in_specs=[pl.BlockSpec((None, tm, d), lambda b, i: (b, i, 0)),
                  pl.BlockSpec((1, d), lambda b, i: (0, 0)),
                  pl.BlockSpec((None, 1, d), lambda b, i: (b, 0, 0)),
                  pl.BlockSpec((None, 1, d), lambda b, i: (b, 0, 0))],
        out_specs=pl.BlockSpec((None, tm, d), lambda b, i: (b, i, 0)),
        out_shape=jax.ShapeDtypeStruct((bsz, seq, d), BF16),
        compiler_params=_cparams(("parallel", "parallel")),
        name="norm_mod",
    )(x, g.reshape(1, d), scale.reshape(bsz, 1, d), shift.reshape(bsz, 1, d))


def _proj_kernel(u_ref, w_ref, o_ref, *, group, gate):
    u = u_ref[...]
    for t in range(group):
        r = _dot(u, w_ref[:, t * TILE_W:(t + 1) * TILE_W])
        if gate:
            r = _sigmoid(r)
        o_ref[t] = r.astype(o_ref.dtype)


def _proj_call(u, w_tiles, layer, *, group, rows, gate, name):
    bsz, seq, d = u.shape
    n_tiles = w_tiles.shape[2] // TILE_W
    tm = min(seq, rows)
    return pl.pallas_call(
        functools.partial(_proj_kernel, group=group, gate=gate),
        grid=(bsz, seq // tm, n_tiles // group),
        in_specs=[pl.BlockSpec((None, tm, d), lambda b, i, j: (b, i, 0)),
                  pl.BlockSpec((None, d, group * TILE_W), lambda b, i, j: (layer, 0, j))],
        out_specs=pl.BlockSpec((group, None, tm, TILE_W), lambda b, i, j: (j, b, i, 0)),
        out_shape=jax.ShapeDtypeStruct((n_tiles, bsz, seq, TILE_W), BF16 if gate else F32),
        compiler_params=_cparams(("parallel", "parallel", "arbitrary")),
        name=name,
    )(u, w_tiles)


def _mix_tiling(seq, want_sub):
    sub_t = min(seq, MIX_T)
    n_sub = want_sub
    while seq % (sub_t * n_sub):
        n_sub //= 2
    return sub_t * n_sub, n_sub


def _sub(ref, s, t):
    return ref.at[pl.ds(s * t, t), :]


def _u_spec(t, d):
    return pl.BlockSpec((None, t, d), lambda b, i: (b, i, 0))


def _run_fused_tiles(tile_fn, widths, u_ref, w_ref, n_sub, t_blk):
    def column_blocks(s):
        blocks, lo = [], 0
        for wd in widths:
            blocks.append(functools.partial(
                lambda lo, wd: _dot(u_ref[pl.ds(s * t_blk, t_blk), :], w_ref[:, lo:lo + wd]), lo, wd))
            lo += wd
        return blocks

    pending = [block() for block in column_blocks(0)]
    for s in range(n_sub):
        current, pending = pending, []
        todo = column_blocks(s + 1) if s + 1 < n_sub else []

        def advance(todo=todo, done=pending):
            if todo:
                done.append(todo.pop(0)())

        tile_fn(s, advance, *current)
        while todo:
            advance()


def _tile_spec(tile_id, t):
    return pl.BlockSpec((None, None, t, TILE_W), lambda b, i: (tile_id, b, i, 0))


def _layer_spec(layer, rows, width):
    return pl.BlockSpec((None, rows, width), lambda b, i: (layer, 0, 0))


def _block_inverse_all(n0s, ri, ci, c):
    eye = jnp.where(ri == ci, 1.0, 0.0)
    lvl1 = (ri >> 1) == (ci >> 1)
    xs = [eye + jnp.where(lvl1, n0, 0.0) for n0 in n0s]
    for lv in range(2, int(np.log2(c)) + 1):
        m = ((ri >> lv) == (ci >> lv)) & ((ri >> (lv - 1)) != (ci >> (lv - 1)))
        xbs = [x.astype(BF16) for x in xs]
        xns = [_dot(xb, jnp.where(m, n0, 0.0).astype(BF16)).astype(BF16) for xb, n0 in zip(xbs, n0s)]
        xs = [x + _dot(xn, xb) for x, xn, xb in zip(xs, xns, xbs)]
    return xs


def _rwkv_kernel(u_ref, w_ref, mu_ref, vec_ref, up_ref, o_ref,
                 carry_ref, st_ref, *, t_blk, n_sub, chunk, pair_t):
    @pl.when(pl.program_id(1) == 0)
    def _():
        carry_ref[...] = jnp.zeros_like(carry_ref)
        st_ref[...] = jnp.zeros_like(st_ref)

    def tile(s, advance, r, k, v, z, codes):
        _rwkv_tile(r, k, v, codes, z, advance, mu_ref, vec_ref, up_ref, _sub(o_ref, s, t_blk), carry_ref, st_ref,
                   t_blk=t_blk, chunk=chunk, pair_t=pair_t)

    _run_fused_tiles(tile, (TILE_W, TILE_W, TILE_W, TILE_W, SMALL_W), u_ref, w_ref, n_sub, t_blk)


def _rwkv_tile(r_ref, k_ref, v_ref, sm_ref, z_ref, advance, mu_ref, vec_ref, up_ref, o_ref,
               carry_ref, st_ref, *, t_blk, chunk, pair_t):
    T, C, TP = t_blk, chunk, pair_t
    n_chunk = T // C
    n_part = T // TP
    c_per_part = TP // C
    for _ in range(5):
        advance()

    row0 = _iota2((T, 1), 0) == 0

    def shift_mix(x, slot, width, mu):
        prev = carry_ref[slot:slot + 1, 0:width]
        xp = jnp.where(row0, prev, pltpu.roll(x, 1, 0))
        carry_ref[slot:slot + 1, 0:width] = x[T - 1:T, :]
        return x + (xp - x) * mu

    r = shift_mix(r_ref[...], 0, TILE_W, mu_ref[0:1, :])
    k = shift_mix(k_ref[...], 1, TILE_W, mu_ref[1:2, :])
    v = shift_mix(v_ref[...], 2, TILE_W, mu_ref[2:3, :])
    codes = shift_mix(sm_ref[:, 0:LANES], 3, LANES, mu_ref[3:4, 0:LANES])

    w0, a0 = vec_ref[0:1, :], vec_ref[1:2, :]
    k_k, k_a, r_k = vec_ref[2:3, :], vec_ref[3:4, :], vec_ref[4:5, :]
    ln_g, ln_b = vec_ref[5:6, :], vec_ref[6:7, :]

    lane = _iota2((T, LANES), 1)
    codes_t = jnp.where(lane < RW_RANK, jnp.tanh(codes), codes).astype(BF16)
    proj = _dot(codes_t, up_ref[...])
    w = -_softplus(-(w0 + proj[:, 0:TILE_W])) - 0.5
    lw = -jnp.exp(w)
    a = _sigmoid(a0 + proj[:, TILE_W:2 * TILE_W])

    ones64 = _seg_ones(TILE_W, RW_N)
    kkr = k * k_k
    ss = _dot((kkr * kkr).astype(BF16), ones64)
    kk = kkr / jnp.maximum(jnp.sqrt(ss), 1e-12)
    k2 = k * (1.0 + (a - 1.0) * k_a)
    alpha = -kk
    beta = kk * a

    ri_t, ci_t, same_t = _chunk_masks(T, C)
    tri = jnp.where(same_t & (ci_t <= ri_t), 1.0, 0.0).astype(BF16)
    blk = jnp.where(same_t, 1.0, 0.0).astype(BF16)
    cw = _mask_dot_left(tri, lw)
    cwl = _mask_dot_left(blk, lw)
    e_in = jnp.exp(cw)
    e_out = jnp.exp(-cw)
    e_end = jnp.exp(cwl - cw)
    a_t = alpha * jnp.exp(cw - lw)
    r_t = r * e_in
    b_t = beta * e_out
    k_t = k2 * e_out
    b_h = (beta * e_end).astype(BF16)
    k_h = (k2 * e_end).astype(BF16)
    p_tot = jnp.exp(cwl)

    a_tt = a_t.T.astype(BF16)
    r_tt = r_t.T.astype(BF16)
    v_tt = v.T.astype(BF16)

    ri, ci, same = _chunk_masks(TP, C)
    up_strict = same & (ci > ri)
    up_incl = same & (ci >= ri)
    lane_head = _iota2((1, LANES), 1) >> 6
    heads = range(RW_H)
    units = [(p, h) for p in range(n_part) for h in heads]

    def hrows(h):
        return slice(h * RW_N, (h + 1) * RW_N)

    m_ba, m_br, m_k = {}, {}, {}
    for u in units:
        p, h = u
        tok = slice(p * TP, (p + 1) * TP)
        win = slice((h // 2) * LANES, (h // 2 + 1) * LANES)
        hmask = lane_head == (h % 2)
        lhs = jnp.concatenate([jnp.where(hmask, b_t[tok, win], 0.0),
                               jnp.where(hmask, k_t[tok, win], 0.0)], axis=0).astype(BF16)
        rhs = jnp.concatenate([a_tt[win, tok], r_tt[win, tok]], axis=1)
        m = _dot(lhs, rhs)
        m_ba[u] = jnp.where(up_strict, m[0:TP, 0:TP], 0.0)
        m_br[u] = jnp.where(up_incl, m[0:TP, TP:2 * TP], 0.0).astype(BF16)
        m_k[u] = jnp.concatenate([jnp.where(up_strict, m[TP:2 * TP, 0:TP], 0.0),
                                  jnp.where(up_incl, m[TP:2 * TP, TP:2 * TP], 0.0)], axis=1).astype(BF16)
    inv = _block_inverse_all([m_ba[u] for u in units], ri, ci, C)
    tts = {u: x.astype(BF16) for u, x in zip(units, inv)}
    g2 = {(p, h): _dot(v_tt[hrows(h), p * TP:(p + 1) * TP], m_k[(p, h)]) for p, h in units}
    wu = {(p, h): _dot(jnp.concatenate([a_tt[hrows(h), p * TP:(p + 1) * TP],
                                        g2[(p, h)][:, 0:TP].astype(BF16)], axis=0), tts[(p, h)])
          for p, h in units}
    w_t = {u: wu[u][0:RW_N, :].astype(BF16) for u in units}
    kv = [[_dot(v_tt[hrows(h), c * C:(c + 1) * C], k_h[c * C:(c + 1) * C, hrows(h)])
           for c in range(n_chunk)] for h in heads]

    st = [st_ref[hrows(h), :] for h in heads]
    yts = [[] for _ in heads]
    for c in range(n_chunk):
        c0 = c * C
        p = c // c_per_part
        cl = (c % c_per_part) * C
        d1 = [_dot(st[h].astype(BF16),
                   jnp.concatenate([w_t[(p, h)][:, cl:cl + C], r_tt[hrows(h), c0:c0 + C]], axis=1))
              for h in heads]
        ut = [(d1[h][:, 0:C] + wu[(p, h)][RW_N:2 * RW_N, cl:cl + C]).astype(BF16) for h in heads]
        d2 = [_dot(ut[h], jnp.concatenate([m_br[(p, h)][cl:cl + C, cl:cl + C],
                                            b_h[c0:c0 + C, hrows(h)]], axis=1))
              for h in heads]
        for h in heads:
            yts[h].append(d1[h][:, C:2 * C] + g2[(p, h)][:, TP + cl:TP + cl + C] + d2[h][:, 0:C])
            st[h] = st[h] * p_tot[c0:c0 + 1, hrows(h)] + d2[h][:, C:C + RW_N] + kv[h][c]
    yt_heads = []
    for h in heads:
        st_ref[hrows(h), :] = st[h]
        yt = jnp.concatenate(yts[h], axis=1)
        d = yt - jnp.mean(yt, axis=0, keepdims=True)
        yt_heads.append(d * lax.rsqrt(jnp.mean(d * d, axis=0, keepdims=True) + RW_GN_EPS))
    y = jnp.concatenate(yt_heads, axis=0).T * ln_g + ln_b
    bonus = _dot((r * k2 * r_k).astype(BF16), ones64) * v
    o_ref[...] = ((y + bonus) * _silu(z_ref[...])).astype(o_ref.dtype)


def _rwkv_call(u, w, layer, mu, vec, up):
    bsz, seq, d = u.shape
    t, n_sub = _mix_tiling(seq, RW_SUB)
    kern = functools.partial(_rwkv_kernel, t_blk=t // n_sub, n_sub=n_sub, chunk=min(RW_CHUNK, t),
                             pair_t=min(RW_PAIR_T, t))
    return pl.pallas_call(
        kern,
        grid=(bsz, seq // t),
        in_specs=[_u_spec(t, d), _layer_spec(layer, d, w.shape[2]),
                  _layer_spec(layer, 8, TILE_W), _layer_spec(layer, 8, TILE_W),
                  _layer_spec(layer, LANES, 2 * TILE_W)],
        out_specs=pl.BlockSpec((None, t, BRANCH_W), lambda b, i: (b, i, 0)),
        out_shape=jax.ShapeDtypeStruct((bsz, seq, BRANCH_W), BF16),
        scratch_shapes=[pltpu.VMEM((8, TILE_W), F32), pltpu.VMEM((RW_H * RW_N, RW_N), F32)],
        compiler_params=_cparams(("parallel", "arbitrary")),
        name="rwkv7_mix",
    )(u, w, mu, vec, up)


def _gla_recurrence(q, k, v, logg, advance, st_ref, qs_ref, ks_ref, vs_ref, gs_ref, os_ref, *,
                    n_head, dk, t_blk, chunk):
    T, C = t_blk, chunk
    n_chunk = T // C
    dv = LANES

    def window(h):
        if dk == LANES:
            return h * LANES, None
        return (h // 2) * LANES, (_iota2((1, LANES), 1) >> 6) == (h % 2)

    ri, ci, same = _chunk_masks(T, C)
    tri = jnp.where(same & (ci <= ri), 1.0, 0.0).astype(BF16)
    blk = jnp.where(same, 1.0, 0.0).astype(BF16)
    b = _mask_dot_left(tri, logg)
    btot = _mask_dot_left(blk, logg)
    worst = jnp.max(jnp.max(-btot, axis=1, keepdims=True), axis=0, keepdims=True)
    heads = range(n_head)
    chunks = range(n_chunk)
    st_in = [st_ref[h * dv:(h + 1) * dv, :] for h in heads]

    advance()
    q_t = (q * jnp.exp(b)).astype(BF16)
    k_t = k * jnp.exp(-b)
    k_h = k * jnp.exp(btot - b)
    d_tot = jnp.exp(btot)
    v_tt = v.T.astype(BF16)
    vb = v.astype(BF16)
    causal = same & (ci <= ri)
    qw, ktw, khw = [], [], []
    for h in heads:
        w0, hm = window(h)
        qw.append(q_t[:, w0:w0 + LANES])
        kt_h, kh_h = k_t[:, w0:w0 + LANES], k_h[:, w0:w0 + LANES]
        if hm is not None:
            kt_h, kh_h = jnp.where(hm, kt_h, 0.0), jnp.where(hm, kh_h, 0.0)
        ktw.append(kt_h.astype(BF16))
        khw.append(kh_h.astype(BF16))
    advance()
    att = [jnp.where(causal, _dot_nt(qw[h], ktw[h]), 0.0).astype(BF16) for h in heads]
    advance()
    ds = [[_dot(v_tt[h * dv:(h + 1) * dv, c * C:(c + 1) * C], khw[h][c * C:(c + 1) * C, :]) for c in chunks]
          for h in heads]
    advance()
    sts = []
    for h in heads:
        w0, _ = window(h)
        st = st_in[h]
        row = []
        for c in chunks:
            row.append(st.astype(BF16))
            st = st * d_tot[c * C:c * C + 1, w0:w0 + LANES] + ds[h][c]
        st_ref[h * dv:(h + 1) * dv, :] = st
        sts.append(row)
    for h in heads:
        inter = [_dot_nt(qw[h][c * C:(c + 1) * C, :], sts[h][c]) for c in chunks]
        os_ref[:, h * dv:(h + 1) * dv] = (_dot(att[h], vb[:, h * dv:(h + 1) * dv])
                                          + jnp.concatenate(inter, axis=0))

    @pl.when(worst[0, 0] > SAFE_EXP)
    def _():
        qs_ref[...] = q
        ks_ref[...] = k
        gs_ref[...] = logg
        vs_ref[...] = v
        eye = _iota2((LANES, LANES), 0) == _iota2((LANES, LANES), 1)
        for h in range(n_head):
            w0, hm = window(h)

            def body(t8, st, w0=w0, hm=hm, h=h):
                rows = pl.ds(pl.multiple_of(t8 * 8, 8), 8)
                q8 = qs_ref[rows, w0:w0 + LANES]
                k8 = ks_ref[rows, w0:w0 + LANES]
                e8 = jnp.exp(gs_ref[rows, w0:w0 + LANES])
                v8 = vs_ref[rows, h * dv:(h + 1) * dv]
                if hm is not None:
                    k8 = jnp.where(hm, k8, 0.0)
                    q8 = jnp.where(hm, q8, 0.0)
                o_rows = []
                for j in range(8):
                    v_col = jnp.sum(jnp.where(eye, v8[j:j + 1, :], 0.0), axis=1, keepdims=True)
                    st = st * e8[j:j + 1, :] + v_col * k8[j:j + 1, :]
                    o_col = jnp.sum(st * q8[j:j + 1, :], axis=1, keepdims=True)
                    o_rows.append(jnp.sum(jnp.where(eye, o_col, 0.0), axis=0, keepdims=True))
                os_ref[rows, h * dv:(h + 1) * dv] = jnp.concatenate(o_rows, axis=0)
                return st

            st_ref[h * dv:(h + 1) * dv, :] = lax.fori_loop(0, T // 8, body, st_in[h])


def _head_rms_gate(o, g, z, n_head):
    outs = []
    for h in range(n_head):
        oh = o[:, h * LANES:(h + 1) * LANES]
        outs.append(oh * lax.rsqrt(jnp.mean(oh * oh, axis=-1, keepdims=True) + NORM_EPS))
    return jnp.concatenate(outs, axis=1) * g * _silu(z)


def _gla_kernel(u_ref, w_ref, up_ref, vec_ref, o_ref, st_ref, *scratch, t_blk, n_sub, chunk):
    @pl.when(pl.program_id(1) == 0)
    def _():
        st_ref[...] = jnp.zeros_like(st_ref)

    def tile(s, advance, qk, v, z, code):
        _gla_tile(qk, v, code, z, advance, up_ref, vec_ref, _sub(o_ref, s, t_blk), st_ref, *scratch,
                  t_blk=t_blk, chunk=chunk)

    _run_fused_tiles(tile, (TILE_W, TILE_W, TILE_W, SMALL_W), u_ref, w_ref, n_sub, t_blk)


def _gla_tile(qk, v, code, z, advance, up_ref, vec_ref, o_ref,
              st_ref, qs_ref, ks_ref, vs_ref, gs_ref, os_ref, *, t_blk, chunk):
    hk = GLA_H * GLA_DK
    q = qk[:, 0:hk] * (GLA_DK ** -0.5)
    k = qk[:, hk:2 * hk]
    logits = _dot(code.astype(BF16), up_ref[...]) + vec_ref[0:1, 0:hk]
    logg = _log_sigmoid(logits) / GLA_LOGIT_NORM
    _gla_recurrence(q, k, v, logg, advance, st_ref, qs_ref, ks_ref, vs_ref, gs_ref, os_ref,
                    n_head=GLA_H, dk=GLA_DK, t_blk=t_blk, chunk=chunk)
    o_ref[...] = _head_rms_gate(os_ref[...], vec_ref[1:2, :], z, GLA_H).astype(o_ref.dtype)


def _hgrn_kernel(u_ref, w_ref, vec_ref, o_ref, st_ref, *scratch, t_blk, n_sub, chunk):
    @pl.when(pl.program_id(1) == 0)
    def _():
        st_ref[...] = jnp.zeros_like(st_ref)

    def tile(s, advance, q, f, i, z):
        _hgrn_tile(q, f, i, z, advance, vec_ref, _sub(o_ref, s, t_blk), st_ref, *scratch,
                   t_blk=t_blk, chunk=chunk)

    _run_fused_tiles(tile, (TILE_W,) * 4, u_ref, w_ref, n_sub, t_blk)


def _hgrn_tile(q, f, i, z, advance, vec_ref, o_ref,
               st_ref, qs_ref, ks_ref, vs_ref, gs_ref, os_ref, *, t_blk, chunk):
    lb = vec_ref[0:1, :]
    g = lb + (1.0 - lb) * _sigmoid(f)
    _gla_recurrence(_silu(q), 1.0 - g, i, jnp.log(g), advance, st_ref,
                    qs_ref, ks_ref, vs_ref, gs_ref, os_ref,
                    n_head=HG_H, dk=LANES, t_blk=t_blk, chunk=chunk)
    o_ref[...] = _head_rms_gate(os_ref[...], vec_ref[1:2, :], z, HG_H).astype(o_ref.dtype)


def _gla_scratch(t, key_w):
    return [pltpu.VMEM((4 * LANES, LANES), F32), pltpu.VMEM((t, key_w), F32), pltpu.VMEM((t, key_w), F32),
            pltpu.VMEM((t, BRANCH_W), F32), pltpu.VMEM((t, key_w), F32), pltpu.VMEM((t, BRANCH_W), F32)]


def _gla_call(u, w, layer, up, vec):
    bsz, seq, d = u.shape
    t, n_sub = _mix_tiling(seq, FUSED_SUB)
    kern = functools.partial(_gla_kernel, t_blk=t // n_sub, n_sub=n_sub, chunk=min(GLA_CHUNK, t))
    return pl.pallas_call(
        kern,
        grid=(bsz, seq // t),
        in_specs=[_u_spec(t, d), _layer_spec(layer, d, w.shape[2]),
                  _layer_spec(layer, LANES, GLA_H * GLA_DK), _layer_spec(layer, 8, TILE_W)],
        out_specs=pl.BlockSpec((None, t, BRANCH_W), lambda b, i: (b, i, 0)),
        out_shape=jax.ShapeDtypeStruct((bsz, seq, BRANCH_W), BF16),
        scratch_shapes=_gla_scratch(t // n_sub, GLA_H * GLA_DK),
        compiler_params=_cparams(("parallel", "arbitrary")),
        name="gla_mix",
    )(u, w, up, vec)


def _hgrn_call(u, w, layer, vec):
    bsz, seq, d = u.shape
    t, n_sub = _mix_tiling(seq, FUSED_SUB)
    chunk = HG_CHUNK_FIRST if layer == 0 else HG_CHUNK
    kern = functools.partial(_hgrn_kernel, t_blk=t // n_sub, n_sub=n_sub, chunk=min(chunk, t))
    return pl.pallas_call(
        kern,
        grid=(bsz, seq // t),
        in_specs=[_u_spec(t, d), _layer_spec(layer, d, w.shape[2]), _layer_spec(layer, 8, TILE_W)],
        out_specs=pl.BlockSpec((None, t, BRANCH_W), lambda b, i: (b, i, 0)),
        out_shape=jax.ShapeDtypeStruct((bsz, seq, BRANCH_W), BF16),
        scratch_shapes=_gla_scratch(t // n_sub, TILE_W),
        compiler_params=_cparams(("parallel", "arbitrary")),
        name="hgrn2_mix",
    )(u, w, vec)


def _mlstm_kernel(u_ref, w_ref, conv_ref, vec_ref, o_ref,
                  prev_ref, ct_ref, n_ref, m_ref, *, t_blk, n_sub, chunk):
    @pl.when(pl.program_id(1) == 0)
    def _():
        prev_ref[...] = jnp.zeros_like(prev_ref)
        ct_ref[...] = jnp.zeros_like(ct_ref)
        n_ref[...] = jnp.zeros_like(n_ref)
        m_ref[...] = jnp.full_like(m_ref, STAB_INIT)

    def tile(s, advance, qk, v, z, gates):
        _mlstm_tile(qk, v, gates, z, advance, conv_ref, vec_ref, _sub(o_ref, s, t_blk), prev_ref, ct_ref, n_ref,
                    m_ref, t_blk=t_blk, chunk=chunk)

    _run_fused_tiles(tile, (TILE_W, TILE_W, TILE_W, SMALL_W), u_ref, w_ref, n_sub, t_blk)


def _mlstm_tile(x, v, gates, z, advance, conv_ref, vec_ref, o_ref,
                prev_ref, ct_ref, n_ref, m_ref, *, t_blk, chunk):
    T, C = t_blk, chunk
    n_chunk = T // C
    heads = range(ML_H)
    for _ in range(4):
        advance()

    xx = jnp.concatenate([prev_ref[...], x], axis=0)
    prev_ref[...] = x[T - 8:T, :]
    conv = x * conv_ref[ML_CONV - 1:ML_CONV, :]
    for d in range(1, ML_CONV):
        conv = conv + xx[8 - d:8 - d + T, :] * conv_ref[ML_CONV - 1 - d:ML_CONV - d, :]
    qk = _silu(conv)
    hq = ML_H * ML_DQK
    qb = qk[:, 0:hq].astype(BF16)
    kf = qk[:, hq:2 * hq] * (ML_DQK ** -0.5)
    kb = kf.astype(BF16)
    vb = v.astype(BF16)
    v_tt = v.T.astype(BF16)

    i_col = gates + vec_ref[0:1, 0:LANES]
    lf_col = _log_sigmoid(gates + vec_ref[1:2, 0:LANES])
    ri_t, ci_t, same_t = _chunk_masks(T, C)
    tri = jnp.where(same_t & (ci_t <= ri_t), 1.0, 0.0).astype(BF16)
    cf_col = _mask_dot_left(tri, lf_col)
    cf_row = cf_col.T
    i_row = i_col.T
    li, lf = ML_I_LANE, ML_F_LANE
    causal = _iota2((C, C), 1) <= _iota2((C, C), 0)
    lane_head = _iota2((1, LANES), 1) >> 6

    ct = [ct_ref[h * LANES:(h + 1) * LANES, :] for h in heads]
    n_row = [n_ref[h:h + 1, :] for h in heads]
    m_prev = [m_ref[h:h + 1, 0:1] for h in heads]
    outs = [[] for _ in heads]
    for c in range(n_chunk):
        tok = slice(c * C, (c + 1) * C)
        for h in heads:
            win = slice((h // 2) * LANES, (h // 2 + 1) * LANES)
            hm = lane_head == (h % 2)
            cf_i = cf_col[tok, lf + h:lf + h + 1]
            i_i = i_col[tok, li + h:li + h + 1]
            cf_j = cf_row[lf + h:lf + h + 1, tok]
            i_j = i_row[li + h:li + h + 1, tok]
            log_d = jnp.where(causal, cf_i - cf_j + i_j, -jnp.inf)
            log_inter = cf_i + m_prev[h]
            m_t = jnp.maximum(log_inter, jnp.max(log_d, axis=-1, keepdims=True))
            qh = jnp.where(hm, qb[tok, win], 0.0)
            s = _dot_nt(qh, kb[tok, win]) * jnp.exp(log_d - m_t)
            w_inter = jnp.exp(log_inter - m_t)
            num = (_dot(s.astype(BF16), vb[tok, h * LANES:(h + 1) * LANES])
                   + w_inter * _dot_nt(qh, ct[h].astype(BF16)))
            den = jnp.sum(s, axis=-1, keepdims=True) + w_inter * jnp.sum(
                jnp.where(hm, qk[tok, win], 0.0) * n_row[h], axis=-1, keepdims=True)
            outs[h].append(num / jnp.maximum(jnp.abs(den), jnp.exp(-m_t)))
            m_new = m_t[C - 1:C, :]
            cf_last = cf_i[C - 1:C, :]
            w_carry = jnp.exp(cf_last + m_prev[h] - m_new)
            kw = jnp.where(hm, kf[tok, win], 0.0) * jnp.exp(cf_last - cf_i + i_i - m_new)
            ct[h] = w_carry * ct[h] + _dot(v_tt[h * LANES:(h + 1) * LANES, tok], kw.astype(BF16))
            n_row[h] = w_carry * n_row[h] + jnp.sum(kw, axis=0, keepdims=True)
            m_prev[h] = m_new

    ln = []
    for h in heads:
        ct_ref[h * LANES:(h + 1) * LANES, :] = ct[h]
        n_ref[h:h + 1, :] = n_row[h]
        m_ref[h:h + 1, :] = jnp.broadcast_to(m_prev[h], (1, LANES))
        oh = jnp.concatenate(outs[h], axis=0)
        d = oh - jnp.mean(oh, axis=-1, keepdims=True)
        ln.append(d * lax.rsqrt(jnp.mean(d * d, axis=-1, keepdims=True) + NORM_EPS))
    o_ref[...] = (jnp.concatenate(ln, axis=1) * vec_ref[2:3, :] * _silu(z)).astype(o_ref.dtype)


def _mlstm_call(u, w, layer, conv_w, vec):
    bsz, seq, d = u.shape
    t, n_sub = _mix_tiling(seq, FUSED_SUB)
    kern = functools.partial(_mlstm_kernel, t_blk=t // n_sub, n_sub=n_sub, chunk=min(ML_CHUNK, t))
    return pl.pallas_call(
        kern,
        grid=(bsz, seq // t),
        in_specs=[_u_spec(t, d), _layer_spec(layer, d, w.shape[2]),
                  _layer_spec(layer, 8, TILE_W), _layer_spec(layer, 8, TILE_W)],
        out_specs=pl.BlockSpec((None, t, BRANCH_W), lambda b, i: (b, i, 0)),
        out_shape=jax.ShapeDtypeStruct((bsz, seq, BRANCH_W), BF16),
        scratch_shapes=[pltpu.VMEM((8, TILE_W), F32), pltpu.VMEM((ML_H * LANES, LANES), F32),
                        pltpu.VMEM((8, LANES), F32), pltpu.VMEM((8, LANES), F32)],
        compiler_params=_cparams(("parallel", "arbitrary")),
        name="mlstm_mix",
    )(u, w, conv_w, vec)


def _merge_kernel(x_ref, ya_ref, yb_ref, yc_ref, yd_ref, *rest, final):
    u_ref, wg_ref, wb_ref, wo_ref, res_gate_ref, ng_ref, nsc_ref, nsh_ref = rest[0:8]
    out_refs = rest[8:]
    d = x_ref.shape[-1]
    u = u_ref[...]
    merged = None
    for m, y_ref in enumerate((ya_ref, yb_ref, yc_ref, yd_ref)):
        br = _dot(y_ref[...], wb_ref[m])
        gates = _sigmoid(_dot(u, wg_ref[:, m * d:(m + 1) * d]))
        term = gates * br
        merged = term if merged is None else merged + term
    x_new = x_ref[...] + res_gate_ref[...] * _dot(merged.astype(BF16), wo_ref[...])
    ms = jnp.mean(x_new * x_new, axis=-1, keepdims=True)
    normed = x_new * lax.rsqrt(ms + NORM_EPS) * ng_ref[...]
    if final:
        out_refs[0][...] = normed
    else:
        out_refs[0][...] = x_new
        out_refs[1][...] = (normed * (1.0 + nsc_ref[...]) + nsh_ref[...]).astype(BF16)


def _merge_call(x, ys, u, w_gate, layer, w_branch, w_out, res_gate, next_g, next_scale, next_shift, final):
    bsz, seq, d = x.shape
    tm = min(seq, 256)
    row = pl.BlockSpec((None, tm, d), lambda b, i: (b, i, 0))
    y_spec = pl.BlockSpec((None, tm, BRANCH_W), lambda b, i: (b, i, 0))
    vec = pl.BlockSpec((None, 1, d), lambda b, i: (b, 0, 0))
    gate_specs = [row, _layer_spec(layer, d, N_BRANCH * d)]
    out_shape = [jax.ShapeDtypeStruct((bsz, seq, d), F32)]
    out_specs = [row]
    if not final:
        out_shape.append(jax.ShapeDtypeStruct((bsz, seq, d), BF16))
        out_specs.append(row)
    return pl.pallas_call(
        functools.partial(_merge_kernel, final=final),
        grid=(bsz, seq // tm),
        in_specs=[row, y_spec, y_spec, y_spec, y_spec] + gate_specs
        + [pl.BlockSpec((None, N_BRANCH, BRANCH_W, d), lambda b, i: (layer, 0, 0, 0)),
           pl.BlockSpec((None, d, d), lambda b, i: (layer, 0, 0)),
           vec, pl.BlockSpec((1, d), lambda b, i: (0, 0)), vec, vec],
        out_specs=out_specs,
        out_shape=out_shape,
        compiler_params=_cparams(("parallel", "parallel")),
        name="merge_out",
    )(x, *ys, u, w_gate, w_branch, w_out, res_gate.reshape(bsz, 1, d),
      next_g.reshape(1, d), next_scale.reshape(bsz, 1, d), next_shift.reshape(bsz, 1, d))


def _pad_last(a, width):
    return jnp.pad(a, [(0, 0)] * (a.ndim - 1) + [(0, width - a.shape[-1])])


def _rows(rows, width, n_rows=8):
    out = [_pad_last(r.astype(F32), width)[:, None, :] for r in rows]
    out.append(jnp.zeros((rows[0].shape[0], n_rows - len(rows), width), F32))
    return jnp.concatenate(out, axis=1)


def _relayout_w_in(w):
    w = w.astype(BF16)
    o = 0

    def take(n):
        nonlocal o
        s = w[:, :, o:o + n]
        o += n
        return s

    rw_r, rw_k, rw_v, rw_wc, rw_ac, rw_z = take(512), take(512), take(512), take(64), take(64), take(512)
    gq, gk, gv, gcode, gz = take(256), take(256), take(512), take(16), take(512)
    mqk, mv, mi, mf, mz = take(512), take(512), take(4), take(4), take(512)
    hq, hf, hi, hz = take(512), take(512), take(512), take(512)
    merge = take(N_BRANCH * D_MODEL)
    lead = jnp.zeros(w.shape[:2] + (ML_I_LANE,), w.dtype)
    blocks = {
        "rw": [rw_r, rw_k, rw_v, rw_z, rw_wc, rw_ac],
        "gla": [gq, gk, gv, gz, _pad_last(gcode, SMALL_W)],
        "ml": [mqk, mv, mz, _pad_last(jnp.concatenate([lead, mi, mf], axis=2), SMALL_W)],
        "hg": [hq, hf, hi, hz],
        "gate": [merge],
    }
    return {name: jnp.concatenate(cols, axis=2) for name, cols in blocks.items()}


def kernel(x, c, norm_g, ada_w, ada_b, w_in, rw_mu, rw_w0, rw_w_up, rw_a0, rw_a_up, rw_k_k, rw_k_a,
           rw_r_k, rw_ln_g, rw_ln_b, gla_gk_up, gla_gk_b, gla_norm_g, ml_conv_w, ml_i_b, ml_f_b,
           ml_norm_g, hg_lb_logits, hg_norm_g, w_branch, w_out, final_g):
    depth = w_in.shape[0]
    d = x.shape[-1]
    lb_p = jax.nn.softmax(hg_lb_logits.astype(F32), axis=0)
    lower_bounds = jnp.cumsum(lb_p, axis=0) - lb_p[0]

    w_cols = _relayout_w_in(w_in)
    rw_mu_rows = _rows([rw_mu[:, 0:512], rw_mu[:, 512:1024], rw_mu[:, 1024:1536], rw_mu[:, 1536:1664]], TILE_W)
    rw_vec = _rows([rw_w0, rw_a0, rw_k_k, rw_k_a, rw_r_k, rw_ln_g, rw_ln_b], TILE_W)
    zeros = jnp.zeros((depth, RW_RANK, BRANCH_W), F32)
    rw_up = jnp.concatenate([jnp.concatenate([rw_w_up, zeros], axis=2),
                             jnp.concatenate([zeros, rw_a_up], axis=2)], axis=1).astype(BF16)
    gla_up = jnp.pad(gla_gk_up, ((0, 0), (0, LANES - GLA_RANK), (0, 0))).astype(BF16)
    gla_vec = _rows([gla_gk_b, gla_norm_g], TILE_W)
    ml_conv = jnp.pad(ml_conv_w.astype(F32), ((0, 0), (0, 8 - ML_CONV), (0, 0)))
    ml_vec = _rows([jnp.pad(ml_i_b, ((0, 0), (ML_I_LANE, 0))),
                    jnp.pad(ml_f_b, ((0, 0), (ML_F_LANE, 0))), ml_norm_g], TILE_W)
    hg_vec = _rows([lower_bounds, hg_norm_g], TILE_W)
    wb = w_branch.astype(BF16)
    wo = w_out.astype(BF16)

    mod = _ada_call(c, ada_w, ada_b)
    u = _norm_call(x, norm_g[0], mod[0, :, d:2 * d], mod[0, :, 0:d])
    for l in range(depth):
        y_a = _rwkv_call(u, w_cols["rw"], l, rw_mu_rows, rw_vec, rw_up)
        y_b = _gla_call(u, w_cols["gla"], l, gla_up, gla_vec)
        y_c = _mlstm_call(u, w_cols["ml"], l, ml_conv, ml_vec)
        y_d = _hgrn_call(u, w_cols["hg"], l, hg_vec)
        final = l == depth - 1
        if final:
            next_g, next_scale, next_shift = final_g, mod[l, :, d:2 * d], mod[l, :, 0:d]
        else:
            next_g, next_scale, next_shift = norm_g[l + 1], mod[l + 1, :, d:2 * d], mod[l + 1, :, 0:d]
        res = _merge_call(x, (y_a, y_b, y_c, y_d), u, w_cols["gate"], l, wb, wo, mod[l, :, 2 * d:3 * d],
                          next_g, next_scale, next_shift, final)
        if final:
            x = res[0]
        else:
            x, u = res
    return x
```
